```python
import math
import jax, jax.numpy as jnp
from jax import lax
import numpy as np

D_MODEL = 1024
BATCH = 4
SEQ = 8192
DEPTH = 2

HEAD_DIM = 64
D_MIX = D_MODEL
HA = (D_MIX // 2) // (2 * HEAD_DIM)
HB = (D_MIX // 4) // HEAD_DIM
HC = (D_MIX // 4) // HEAD_DIM
WA = HA * 2 * HEAD_DIM
WB = HB * HEAD_DIM
WC = HC * HEAD_DIM
ROT_DIM = HEAD_DIM // 4
ROPE_THETA = 500000.0
Q_BLOCK = 128
LN_EPS = 1e-5
SUBLN_EPS = 1e-5
DEEPNORM_ALPHA = (2 * DEPTH) ** 0.25
DEEPNORM_BETA = (8 * DEPTH) ** -0.25
SPLIT_SIZES = (WA, WA, WA, WA, WB, WB, WB, HB, WB, WC, WC, WC, WC)
VALUE_SEGMENTS = (2, 6, 11)
D_IN = sum(SPLIT_SIZES)
SPLIT_POINTS = tuple(int(v) for v in np.cumsum(SPLIT_SIZES)[:-1])

kernel_name = 'hybrid_diff_fox_stickbreak_heads'


def _layernorm(t, g, b):
    tf = t.astype(jnp.float32)
    mu = jnp.mean(tf, -1, keepdims=True)
    var = jnp.mean(jnp.square(tf - mu), -1, keepdims=True)
    out = (tf - mu) * lax.rsqrt(var + LN_EPS) * g.astype(jnp.float32) + b.astype(jnp.float32)
    return out.astype(t.dtype)


def _rope_tables(seq):
    inv = ROPE_THETA ** (-jnp.arange(0, ROT_DIM, 2, dtype=jnp.float32) / ROT_DIM)
    ang = jnp.arange(seq, dtype=jnp.float32)[:, None] * inv[None, :]
    return jnp.cos(ang), jnp.sin(ang)


def _partial_rope(t, cos, sin):
    half = ROT_DIM // 2
    x1 = t[..., :half].astype(jnp.float32)
    x2 = t[..., half:ROT_DIM].astype(jnp.float32)
    r1 = (x1 * cos - x2 * sin).astype(t.dtype)
    r2 = (x2 * cos + x1 * sin).astype(t.dtype)
    return jnp.concatenate([r1, r2, t[..., ROT_DIM:]], axis=-1)


def _split_heads(t, n_heads):
    b_, s_, w_ = t.shape
    return t.reshape(b_, s_, n_heads, w_ // n_heads).transpose(0, 2, 1, 3)


def _to_blocks(t):
    b_, h_, s_ = t.shape[:3]
    t = t.reshape((b_, h_, s_ // Q_BLOCK, Q_BLOCK) + t.shape[3:])
    return jnp.moveaxis(t, 2, 0)


def _from_blocks(t):
    nb, b_, h_, blk, dv = t.shape
    return t.transpose(1, 0, 3, 2, 4).reshape(b_, nb * blk, h_ * dv)


def _differential_attention(q1, q2, k1, k2, v, lam):
    seq = k1.shape[2]
    kpos = jnp.arange(seq)
    starts = jnp.arange(seq // Q_BLOCK) * Q_BLOCK
    scale = HEAD_DIM ** -0.5

    def block(args):
        q1b, q2b, start = args
        qpos = start + jnp.arange(Q_BLOCK)
        causal = kpos[None, :] <= qpos[:, None]
        s1 = jnp.einsum('bhqd,bhkd->bhqk', q1b, k1).astype(jnp.float32) * scale
        s2 = jnp.einsum('bhqd,bhkd->bhqk', q2b, k2).astype(jnp.float32) * scale
        p1 = jax.nn.softmax(jnp.where(causal, s1, -jnp.inf), axis=-1)
        p2 = jax.nn.softmax(jnp.where(causal, s2, -jnp.inf), axis=-1)
        w = (p1 - lam * p2).astype(v.dtype)
        return jnp.einsum('bhqk,bhkd->bhqd', w, v)

    return lax.map(block, (_to_blocks(q1), _to_blocks(q2), starts))


def _forgetting_attention(q, k, v, cum_logf):
    seq = k.shape[2]
    kpos = jnp.arange(seq)
    starts = jnp.arange(seq // Q_BLOCK) * Q_BLOCK
    scale = HEAD_DIM ** -0.5

    def block(args):
        qb, cqb, start = args
        qpos = start + jnp.arange(Q_BLOCK)
        causal = kpos[None, :] <= qpos[:, None]
        s = jnp.einsum('bhqd,bhkd->bhqk', qb, k).astype(jnp.float32) * scale
        logits = s + cqb[..., :, None] - cum_logf[..., None, :]
        p = jax.nn.softmax(jnp.where(causal, logits, -jnp.inf), axis=-1)
        return jnp.einsum('bhqk,bhkd->bhqd', p.astype(v.dtype), v)

    return lax.map(block, (_to_blocks(q), _to_blocks(cum_logf), starts))


def _stick_breaking_attention(q, k, v):
    seq = k.shape[2]
    kpos = jnp.arange(seq)
    starts = jnp.arange(seq // Q_BLOCK) * Q_BLOCK
    scale = HEAD_DIM ** -0.5

    def block(args):
        qb, start = args
        qpos = start + jnp.arange(Q_BLOCK)
        strict = kpos[None, :] < qpos[:, None]
        z = jnp.einsum('bhqd,bhkd->bhqk', qb, k).astype(jnp.float32) * scale
        log_fail = jnp.where(strict, jax.nn.log_sigmoid(-z), 0.0)
        after = lax.cumsum(log_fail, axis=3, reverse=True) - log_fail
        weights = jnp.where(strict, jnp.exp(jax.nn.log_sigmoid(z) + after), 0.0)
        return jnp.einsum('bhqk,bhkd->bhqd', weights.astype(v.dtype), v)

    return lax.map(block, (_to_blocks(q), starts))


def setup_inputs(seed: int = 0) -> dict:
    key = jax.random.key(seed)
    ks = jax.random.split(key, 13)
    x = jax.random.normal(ks[0], (BATCH, SEQ, D_MODEL), jnp.float32)
    ln_in_g = 1.0 + 0.02 * jax.random.normal(ks[1], (D_MODEL,), jnp.float32)
    ln_in_b = 0.02 * jax.random.normal(ks[2], (D_MODEL,), jnp.float32)
    col_scale = jnp.concatenate([
        jnp.full((n,), DEEPNORM_BETA if i in VALUE_SEGMENTS else 1.0, jnp.float32)
        for i, n in enumerate(SPLIT_SIZES)])
    w_in = (jax.random.normal(ks[3], (DEPTH, D_MODEL, D_IN), jnp.float32)
            * (D_MODEL ** -0.5) * col_scale)
    b_forget = 1.0 + 0.1 * jax.random.normal(ks[4], (DEPTH, HB), jnp.float32)
    lambda_q1 = 0.1 * jax.random.normal(ks[5], (DEPTH, HEAD_DIM), jnp.float32)
    lambda_k1 = 0.1 * jax.random.normal(ks[6], (DEPTH, HEAD_DIM), jnp.float32)
    lambda_q2 = 0.1 * jax.random.normal(ks[7], (DEPTH, HEAD_DIM), jnp.float32)
    lambda_k2 = 0.1 * jax.random.normal(ks[8], (DEPTH, HEAD_DIM), jnp.float32)
    subln_g = 1.0 + 0.02 * jax.random.normal(ks[9], (DEPTH, 2 * HEAD_DIM), jnp.float32)
    w_out = (jax.random.normal(ks[10], (DEPTH, D_MIX, D_MODEL), jnp.float32)
             * (D_MIX ** -0.5) * DEEPNORM_BETA)
    ln_g = 1.0 + 0.02 * jax.random.normal(ks[11], (DEPTH, D_MODEL), jnp.float32)
    ln_b = 0.02 * jax.random.normal(ks[12], (DEPTH, D_MODEL), jnp.float32)
    return {'x': x, 'ln_in_g': ln_in_g, 'ln_in_b': ln_in_b, 'w_in': w_in,
            'b_forget': b_forget, 'lambda_q1': lambda_q1, 'lambda_k1': lambda_k1,
            'lambda_q2': lambda_q2, 'lambda_k2': lambda_k2, 'subln_g': subln_g,
            'w_out': w_out, 'ln_g': ln_g, 'ln_b': ln_b}


def reference(x, ln_in_g, ln_in_b, w_in, b_forget, lambda_q1, lambda_k1, lambda_q2,
              lambda_k2, subln_g, w_out, ln_g, ln_b):
    bsz, seq, _ = x.shape
    cos, sin = _rope_tables(seq)
    h = _layernorm(x, ln_in_g, ln_in_b)
    for layer in range(DEPTH):
        proj = jnp.einsum('bsd,de->bse', h, w_in[layer])
        (aq, ak, av, ag, bq, bk, bv, bf, bg, cq, ck, cv, cg) = jnp.split(proj, SPLIT_POINTS, axis=-1)

        qa = _partial_rope(_split_heads(aq, 2 * HA), cos, sin).reshape(bsz, HA, 2, seq, HEAD_DIM)
        ka = _partial_rope(_split_heads(ak, 2 * HA), cos, sin).reshape(bsz, HA, 2, seq, HEAD_DIM)
        va = _split_heads(av, HA)
        lam_init = 0.8 - 0.6 * math.exp(-0.3 * layer)
        lam = (jnp.exp(jnp.sum(lambda_q1[layer].astype(jnp.float32) * lambda_k1[layer].astype(jnp.float32)))
               - jnp.exp(jnp.sum(lambda_q2[layer].astype(jnp.float32) * lambda_k2[layer].astype(jnp.float32)))
               + lam_init)
        oa = _differential_attention(qa[:, :, 0], qa[:, :, 1], ka[:, :, 0], ka[:, :, 1], va, lam)
        oaf = oa.astype(jnp.float32)
        oa = (oaf * lax.rsqrt(jnp.mean(jnp.square(oaf), -1, keepdims=True) + SUBLN_EPS)
              * subln_g[layer].astype(jnp.float32) * (1.0 - lam_init)).astype(h.dtype)
        ya = _from_blocks(oa) * jax.nn.silu(ag)

        logf = jax.nn.log_sigmoid((bf + b_forget[layer]).astype(jnp.float32))
        cum_logf = jnp.cumsum(logf, axis=1).transpose(0, 2, 1)
        ob = _forgetting_attention(_split_heads(bq, HB), _split_heads(bk, HB),
                                   _split_heads(bv, HB), cum_logf)
        yb = _from_blocks(ob) * jax.nn.silu(bg)

        oc = _stick_breaking_attention(_split_heads(cq, HC), _split_heads(ck, HC), _split_heads(cv, HC))
        yc = _from_blocks(oc) * jax.nn.silu(cg)

        y = jnp.einsum('bse,ed->bsd', jnp.concatenate([ya, yb, yc], axis=-1), w_out[layer])
        h = _layernorm(DEEPNORM_ALPHA * h + y, ln_g[layer], ln_b[layer])
    return h
```

```python
import functools
import math

import jax
import jax.numpy as jnp
from jax import lax
from jax.experimental import pallas as pl
from jax.experimental.pallas import tpu as pltpu

HEAD_DIM = 64
LANES = 128
ROT_DIM = HEAD_DIM // 4
ROPE_THETA = 500000.0
LN_EPS = 1e-5
SUBLN_EPS = 1e-5
LOG2E = 1.4426950408889634
NEG_BIG = -1e30
SKIP_NATS = 100.0
VMEM_LIMIT = 56 * 1024 * 1024

_NT = (((1,), (1,)), ((), ()))


def _dot(a, b):
    return jnp.dot(a, b, preferred_element_type=jnp.float32)


def _dot_nt(a, b):
    return lax.dot_general(a, b, _NT, preferred_element_type=jnp.float32)


def _layernorm_rows(t, g, b):
    mu = jnp.mean(t, axis=-1, keepdims=True)
    d = t - mu
    var = jnp.mean(d * d, axis=-1, keepdims=True)
    return d * lax.rsqrt(var + LN_EPS) * g + b


def _ln_kernel(x_ref, g_ref, b_ref, o_ref):
    o_ref[...] = _layernorm_rows(x_ref[...], g_ref[...], b_ref[...])


def _layernorm(x2d, g, b, tm):
    m, d = x2d.shape
    return pl.pallas_call(
        _ln_kernel,
        grid=(m // tm,),
        in_specs=[pl.BlockSpec((tm, d), lambda i: (i, 0)),
                  pl.BlockSpec((1, d), lambda i: (0, 0)),
                  pl.BlockSpec((1, d), lambda i: (0, 0))],
        out_specs=pl.BlockSpec((tm, d), lambda i: (i, 0)),
        out_shape=jax.ShapeDtypeStruct((m, d), jnp.float32),
        compiler_params=pltpu.CompilerParams(vmem_limit_bytes=VMEM_LIMIT),
        name="ln_in",
    )(x2d, g.reshape(1, d), b.reshape(1, d))


def _rope(t, cos, sin_hi, sin_lo):
    out = []
    for c in range(t.shape[1] // LANES):
        x = t[:, c * LANES:(c + 1) * LANES]
        out.append(x * cos
                   + pltpu.roll(x, ROT_DIM // 2, axis=1) * sin_hi
                   + pltpu.roll(x, LANES - ROT_DIM // 2, axis=1) * sin_lo)
    return jnp.concatenate(out, axis=1)


def _silu(t):
    return t * jax.nn.sigmoid(t)


def _log_sigmoid(t):
    return jnp.minimum(t, 0.0) - jnp.log1p(jnp.exp(-jnp.abs(t)))


def _inproj_kernel(h_ref, wa_ref, wb_ref, wf_ref, wc_ref, bf_ref, cos_ref, shi_ref, slo_ref, tri_ref,
                   qa_ref, ka_ref, va_ref, ga_ref, qb_ref, kb_ref, vb_ref, gb_ref, cb_ref,
                   qc_ref, kc_ref, vc_ref, gc_ref, carry_ref, *, wa, wb, wc):
    scale = HEAD_DIM ** -0.5
    hb = h_ref[0].astype(jnp.bfloat16)
    cos, shi, slo = cos_ref[...], shi_ref[...], slo_ref[...]
    bf16 = jnp.bfloat16

    qa_ref[0] = (_rope(_dot(hb, wa_ref[:, 0:wa]), cos, shi, slo) * (scale * LOG2E)).astype(bf16)
    ka_ref[0] = _rope(_dot(hb, wa_ref[:, wa:2 * wa]), cos, shi, slo).astype(bf16)
    va_ref[0] = _dot(hb, wa_ref[:, 2 * wa:3 * wa]).astype(bf16)
    ga_ref[0] = _silu(_dot(hb, wa_ref[:, 3 * wa:4 * wa])).astype(bf16)

    qb_ref[0] = (_dot(hb, wb_ref[:, 0:wb]) * (scale * LOG2E)).astype(bf16)
    kb_ref[0] = _dot(hb, wb_ref[:, wb:2 * wb]).astype(bf16)
    vb_ref[0] = _dot(hb, wb_ref[:, 2 * wb:3 * wb]).astype(bf16)
    gb_ref[0] = _silu(_dot(hb, wb_ref[:, 3 * wb:4 * wb])).astype(bf16)

    @pl.when(pl.program_id(1) == 0)
    def _():
        carry_ref[...] = jnp.zeros_like(carry_ref)

    logf = _log_sigmoid(_dot(hb, wf_ref[...]) + bf_ref[...])
    p1 = logf.astype(bf16)
    r1 = logf - p1.astype(jnp.float32)
    p2 = r1.astype(bf16)
    p3 = (r1 - p2.astype(jnp.float32)).astype(bf16)
    cs = _dot(tri_ref[...], jnp.concatenate([p1, p2, p3], axis=1))
    c = cs[:, 0:LANES] + cs[:, LANES:2 * LANES] + cs[:, 2 * LANES:3 * LANES] + carry_ref[...]
    carry_ref[...] = c[c.shape[0] - 1:, :]
    cb_ref[0] = c * LOG2E

    qc_ref[0] = (_dot(hb, wc_ref[:, 0:wc]) * scale).astype(bf16)
    kc_ref[0] = _dot(hb, wc_ref[:, wc:2 * wc]).astype(bf16)
    vc_ref[0] = _dot(hb, wc_ref[:, 2 * wc:3 * wc]).astype(bf16)
    gc_ref[0] = _silu(_dot(hb, wc_ref[:, 3 * wc:4 * wc])).astype(bf16)


def _inproj(h, w_a, w_b, w_f, w_c, b_f, cos, shi, slo, tri, tm):
    bsz, seq, d = h.shape
    wa, wb, wc = w_a.shape[1] // 4, w_b.shape[1] // 4, w_c.shape[1] // 4
    row = lambda n: pl.BlockSpec((1, tm, n), lambda b, t: (b, t, 0))
    full = lambda a: pl.BlockSpec(a.shape, lambda b, t: (0,) * a.ndim)
    tab = pl.BlockSpec((tm, LANES), lambda b, t: (t, 0))
    bf16 = jnp.bfloat16
    shp = lambda n, dt=bf16: jax.ShapeDtypeStruct((bsz, seq, n), dt)
    return pl.pallas_call(
        functools.partial(_inproj_kernel, wa=wa, wb=wb, wc=wc),
        grid=(bsz, seq // tm),
        in_specs=[row(d), full(w_a), full(w_b), full(w_f), full(w_c), full(b_f), tab, tab, tab, full(tri)],
        out_specs=[row(wa)] * 4 + [row(wb)] * 4 + [row(LANES)] + [row(wc)] * 4,
        out_shape=[shp(wa)] * 4 + [shp(wb)] * 4 + [shp(LANES, jnp.float32)] + [shp(wc)] * 4,
        scratch_shapes=[pltpu.VMEM((1, LANES), jnp.float32)],
        compiler_params=pltpu.CompilerParams(
            dimension_semantics=("arbitrary", "arbitrary"), vmem_limit_bytes=VMEM_LIMIT),
        name="in_proj",
    )(h, w_a, w_b, w_f, w_c, b_f, cos, shi, slo, tri)


def _head_masks(rows):
    lane = lax.broadcasted_iota(jnp.int32, (rows, LANES), 1)
    return lane < HEAD_DIM


def _causal(rows, cols, strict):
    r = lax.broadcasted_iota(jnp.int32, (rows, cols), 0)
    c = lax.broadcasted_iota(jnp.int32, (rows, cols), 1)
    return (c < r) if strict else (c <= r)


def _softmax_step(s, v, m_ref, l_ref, acc_ref, h):
    m_old = m_ref[h]
    m_new = jnp.maximum(m_old, jnp.max(s, axis=1, keepdims=True))
    alpha = jnp.exp2(m_old - m_new)
    p = jnp.exp2(s - m_new)
    l_ref[h] = alpha * l_ref[h] + jnp.sum(p, axis=1, keepdims=True)
    acc_ref[h] = alpha * acc_ref[h] + _dot(p.astype(jnp.bfloat16), v)
    m_ref[h] = m_new


def _attn_a_kernel(lam_ref, sg_ref, q_ref, k_ref, v_ref, g_ref, o_ref, m_ref, l_ref, acc_ref,
                   *, blk, lam_init):
    seq = q_ref.shape[1]
    low = _head_masks(blk)
    diag = _causal(blk, blk, strict=False)
    lv = lam_ref[...]
    lam = (jnp.exp(jnp.sum(lv[0:1] * lv[1:2], axis=1, keepdims=True))
           - jnp.exp(jnp.sum(lv[2:3] * lv[3:4], axis=1, keepdims=True)) + lam_init)

    def q_block(i, _):
        rows = pl.ds(pl.multiple_of(i * blk, blk), blk)
        q = q_ref[0, rows, :]
        zero = jnp.zeros_like(q)
        qs = (jnp.where(low, q, zero), jnp.where(low, zero, q))
        m_ref[...] = jnp.full_like(m_ref, NEG_BIG)
        l_ref[...] = jnp.zeros_like(l_ref)
        acc_ref[...] = jnp.zeros_like(acc_ref)

        def step(j, masked):
            cols = pl.ds(pl.multiple_of(j * blk, blk), blk)
            k = k_ref[0, cols, :]
            v = v_ref[0, cols, :]
            for h in range(2):
                s = _dot_nt(qs[h], k)
                if masked:
                    s = jnp.where(diag, s, NEG_BIG)
                _softmax_step(s, v, m_ref, l_ref, acc_ref, h)

        lax.fori_loop(0, i, lambda j, c: (step(j, False), c)[1], 0)
        step(i, True)

        o = acc_ref[0] / l_ref[0] - lam * (acc_ref[1] / l_ref[1])
        o = o * lax.rsqrt(jnp.mean(o * o, axis=1, keepdims=True) + SUBLN_EPS)
        o = o * (sg_ref[...] * (1.0 - lam_init))
        o_ref[0, rows, :] = (o * g_ref[0, rows, :].astype(jnp.float32)).astype(o_ref.dtype)
        return 0

    lax.fori_loop(0, seq // blk, q_block, 0)


def _attn_a(lam_vecs, subln_g, qa, ka, va, ga, lam_init, blk):
    bsz, seq, width = qa.shape
    heads = width // LANES
    spec = pl.BlockSpec((1, seq, LANES), lambda b, h: (b, 0, h))
    small = lambda a: pl.BlockSpec(a.shape, lambda b, h: (0, 0))
    return pl.pallas_call(
        functools.partial(_attn_a_kernel, blk=blk, lam_init=lam_init),
        grid=(bsz, heads),
        in_specs=[small(lam_vecs), small(subln_g), spec, spec, spec, spec],
        out_specs=spec,
        out_shape=jax.ShapeDtypeStruct((bsz, seq, width), jnp.bfloat16),
        scratch_shapes=[pltpu.VMEM((2, blk, 1), jnp.float32), pltpu.VMEM((2, blk, 1), jnp.float32),
                        pltpu.VMEM((2, blk, LANES), jnp.float32)],
        compiler_params=pltpu.CompilerParams(
            dimension_semantics=("arbitrary", "arbitrary"), vmem_limit_bytes=VMEM_LIMIT),
        name="attn_diff",
    )(lam_vecs, subln_g, qa, ka, va, ga)


def _attn_b_kernel(q_ref, k_ref, v_ref, g_ref, cc_ref, cr_ref, o_ref, m_ref, l_ref, acc_ref, kn_ref,
                   *, blk):
    seq = q_ref.shape[1]
    pair = pl.program_id(1)
    low = _head_masks(blk)
    diag = _causal(blk, blk, strict=False)
    lane = lax.broadcasted_iota(jnp.int32, (1, LANES), 1)
    f32 = jnp.float32

    sel = (lax.broadcasted_iota(jnp.int32, (LANES, LANES), 0) // HEAD_DIM
           == lax.broadcasted_iota(jnp.int32, (LANES, LANES), 1)).astype(jnp.bfloat16)
    kn_ref[...] = jnp.zeros_like(kn_ref)

    def kn_block(j, _):
        kk = k_ref[0, pl.ds(pl.multiple_of(j * blk, blk), blk), :].astype(f32)
        n2 = _dot((kk * kk).astype(jnp.bfloat16), sel)
        kn_ref[...] = jnp.maximum(kn_ref[...], jnp.max(n2, axis=0, keepdims=True))
        return 0

    lax.fori_loop(0, seq // blk, kn_block, 0)
    kn = jnp.sqrt(kn_ref[...]) * 1.01

    def q_block(i, _):
        rows = pl.ds(pl.multiple_of(i * blk, blk), blk)
        q = q_ref[0, rows, :]
        zero = jnp.zeros_like(q)
        qs = (jnp.where(low, q, zero), jnp.where(low, zero, q))
        qf = q.astype(f32)
        qn2 = _dot((qf * qf).astype(jnp.bfloat16), sel)
        cc = cc_ref[0, rows, :]
        reach, cq = [], []
        for h in range(2):
            hsel = lane == h
            qn_h = jnp.sqrt(jnp.sum(jnp.where(hsel, qn2, 0.0), axis=1, keepdims=True)) * 1.01
            kn_h = jnp.sum(jnp.where(hsel, kn, 0.0), axis=1, keepdims=True)
            cq_h = jnp.sum(jnp.where(lane == 2 * pair + h, cc, 0.0), axis=1, keepdims=True)
            reach.append(qn_h * kn_h + cq_h)
            cq.append(cq_h)
        m_ref[...] = jnp.full_like(m_ref, NEG_BIG)
        l_ref[...] = jnp.zeros_like(l_ref)
        acc_ref[...] = jnp.zeros_like(acc_ref)

        def step(j, masked):
            cols = pl.ds(pl.multiple_of(j * blk, blk), blk)
            k = k_ref[0, cols, :]
            v = v_ref[0, cols, :]
            alive = jnp.zeros((1, 1), f32)
            for h in range(2):
                ck = cr_ref[0, 0, h:h + 1, cols]
                s = _dot_nt(qs[h], k) + (cq[h] - ck)
                if masked:
                    s = jnp.where(diag, s, NEG_BIG)
                _softmax_step(s, v, m_ref, l_ref, acc_ref, h)
                bound = reach[h] - jnp.max(ck, axis=1, keepdims=True)
                live = jnp.max(bound - m_ref[h], axis=0, keepdims=True) >= -SKIP_NATS * LOG2E
                alive = jnp.maximum(alive, live.astype(f32))
            return jnp.max(alive) > 0.0

        go = step(i, True)

        def cond(c):
            return jnp.logical_and(c[0] >= 0, c[1])

        def body(c):
            return c[0] - 1, step(c[0], False)

        lax.while_loop(cond, body, (i - 1, go))

        o = jnp.where(low, acc_ref[0] / l_ref[0], acc_ref[1] / l_ref[1])
        o_ref[0, rows, :] = (o * g_ref[0, rows, :].astype(f32)).astype(o_ref.dtype)
        return 0

    lax.fori_loop(0, seq // blk, q_block, 0)


def _attn_b(qb, kb, vb, gb, c_col, c_row, blk):
    bsz, seq, width = qb.shape
    pairs = width // LANES
    spec = pl.BlockSpec((1, seq, LANES), lambda b, h: (b, 0, h))
    return pl.pallas_call(
        functools.partial(_attn_b_kernel, blk=blk),
        grid=(bsz, pairs),
        in_specs=[spec, spec, spec, spec,
                  pl.BlockSpec((1, seq, LANES), lambda b, h: (b, 0, 0)),
                  pl.BlockSpec((1, 1) + c_row.shape[2:], lambda b, h: (b, h, 0, 0))],
        out_specs=spec,
        out_shape=jax.ShapeDtypeStruct((bsz, seq, width), jnp.bfloat16),
        scratch_shapes=[pltpu.VMEM((2, blk, 1), jnp.float32), pltpu.VMEM((2, blk, 1), jnp.float32),
                        pltpu.VMEM((2, blk, LANES), jnp.float32), pltpu.VMEM((1, LANES), jnp.float32)],
        compiler_params=pltpu.CompilerParams(
            dimension_semantics=("arbitrary", "arbitrary"), vmem_limit_bytes=VMEM_LIMIT),
        name="attn_forget",
    )(qb, kb, vb, gb, c_col, c_row)


def _attn_c_kernel(q_ref, k_ref, v_ref, g_ref, o_ref, run_ref, acc_ref, *, blk):
    seq = q_ref.shape[1]
    low = _head_masks(blk)
    strict = _causal(blk, blk, strict=True)
    f32, bf16 = jnp.float32, jnp.bfloat16
    jj = lax.broadcasted_iota(jnp.int32, (2 * blk, blk), 0) % blk
    ss = lax.broadcasted_iota(jnp.int32, (2 * blk, blk), 1)
    after_op = (jj > ss).astype(bf16)

    def q_block(i, _):
        rows = pl.ds(pl.multiple_of(i * blk, blk), blk)
        q = q_ref[0, rows, :]
        zero = jnp.zeros_like(q)
        qs = (jnp.where(low, q, zero), jnp.where(low, zero, q))
        run_ref[...] = jnp.zeros_like(run_ref)
        acc_ref[...] = jnp.zeros_like(acc_ref)

        def step(j, masked):
            cols = pl.ds(pl.multiple_of(j * blk, blk), blk)
            k = k_ref[0, cols, :]
            v = v_ref[0, cols, :]
            alive = jnp.zeros((1, 1), f32)
            for h in range(2):
                z = _dot_nt(qs[h], k)
                log_fail = -(jnp.maximum(z, 0.0) + jnp.log1p(jnp.exp(-jnp.abs(z))))
                log_hit = z + log_fail
                if masked:
                    log_fail = jnp.where(strict, log_fail, 0.0)
                hi = log_fail.astype(bf16)
                lo = (log_fail - hi.astype(f32)).astype(bf16)
                after = _dot(jnp.concatenate([hi, lo], axis=1), after_op) + run_ref[h]
                w = jnp.exp(log_hit + after)
                if masked:
                    w = jnp.where(strict, w, 0.0)
                acc_ref[h] = acc_ref[h] + _dot(w.astype(bf16), v)
                run = run_ref[h] + jnp.sum(log_fail, axis=1, keepdims=True)
                run_ref[h] = run
                live = jnp.max(run, axis=0, keepdims=True) >= -SKIP_NATS
                alive = jnp.maximum(alive, live.astype(f32))
            return jnp.max(alive) > 0.0

        go = step(i, True)

        def cond(c):
            return jnp.logical_and(c[0] >= 0, c[1])

        def body(c):
            return c[0] - 1, step(c[0], False)

        lax.while_loop(cond, body, (i - 1, go))

        o = jnp.where(low, acc_ref[0], acc_ref[1])
        o_ref[0, rows, :] = (o * g_ref[0, rows, :].astype(f32)).astype(o_ref.dtype)
        return 0

    lax.fori_loop(0, seq // blk, q_block, 0)


def _attn_c(qc, kc, vc, gc, blk):
    bsz, seq, width = qc.shape
    pairs = width // LANES
    spec = pl.BlockSpec((1, seq, LANES), lambda b, h: (b, 0, h))
    return pl.pallas_call(
        functools.partial(_attn_c_kernel, blk=blk),
        grid=(bsz, pairs),
        in_specs=[spec, spec, spec, spec],
        out_specs=spec,
        out_shape=jax.ShapeDtypeStruct((bsz, seq, width), jnp.bfloat16),
        scratch_shapes=[pltpu.VMEM((2, blk, 1), jnp.float32), pltpu.VMEM((2, blk, LANES), jnp.float32)],
        compiler_params=pltpu.CompilerParams(
            dimension_semantics=("arbitrary", "arbitrary"), vmem_limit_bytes=VMEM_LIMIT),
        name="attn_stick",
    )(qc, kc, vc, gc)


def _outproj_kernel(ya_ref, yb_ref, yc_ref, h_ref, w_ref, g_ref, b_ref, o_ref, *, alpha):
    wa, wb = ya_ref.shape[1], yb_ref.shape[1]
    y = (_dot(ya_ref[...], w_ref[0:wa, :]) + _dot(yb_ref[...], w_ref[wa:wa + wb, :])
         + _dot(yc_ref[...], w_ref[wa + wb:, :]))
    o_ref[...] = _layernorm_rows(alpha * h_ref[...] + y, g_ref[...], b_ref[...])


def _outproj(ya, yb, yc, h2d, w, g, b, alpha, tm):
    m, d = h2d.shape
    row = lambda a: pl.BlockSpec((tm, a.shape[1]), lambda i: (i, 0))
    full = lambda a: pl.BlockSpec(a.shape, lambda i: (0, 0))
    return pl.pallas_call(
        functools.partial(_outproj_kernel, alpha=alpha),
        grid=(m // tm,),
        in_specs=[row(ya), row(yb), row(yc), row(h2d), full(w), full(g), full(b)],
        out_specs=pl.BlockSpec((tm, d), lambda i: (i, 0)),
        out_shape=jax.ShapeDtypeStruct((m, d), jnp.float32),
        compiler_params=pltpu.CompilerParams(vmem_limit_bytes=VMEM_LIMIT),
        name="out_proj_ln",
    )(ya, yb, yc, h2d, w, g, b)


def _rope_tables(seq):
    half = ROT_DIM // 2
    inv = ROPE_THETA ** (-jnp.arange(0, ROT_DIM, 2, dtype=jnp.float32) / ROT_DIM)
    ang = jnp.arange(seq, dtype=jnp.float32)[:, None] * inv[None, :]
    cos, sin = jnp.cos(ang), jnp.sin(ang)
    pad = HEAD_DIM - ROT_DIM
    one_head = lambda a, b, fill: jnp.concatenate(
        [a, b, jnp.full((seq, pad), fill, jnp.float32)], axis=1)
    zeros = jnp.zeros((seq, half), jnp.float32)
    reps = LANES // HEAD_DIM
    cos_t = jnp.tile(one_head(cos, cos, 1.0), (1, reps))
    sin_hi = jnp.tile(one_head(zeros, sin, 0.0), (1, reps))
    sin_lo = jnp.tile(one_head(-sin, zeros, 0.0), (1, reps))
    return cos_t, sin_hi, sin_lo


def kernel(x, ln_in_g, ln_in_b, w_in, b_forget, lambda_q1, lambda_k1, lambda_q2, lambda_k2,
           subln_g, w_out, ln_g, ln_b):
    bsz, seq, d = x.shape
    depth = w_in.shape[0]
    hb = b_forget.shape[1]
    d_mix = w_out.shape[1]
    wa, wb, wc = d_mix // 2, d_mix // 4, d_mix // 4
    assert w_in.shape[2] == 4 * wa + 4 * wb + hb + 4 * wc and wb == hb * HEAD_DIM
    alpha = (2 * depth) ** 0.25
    tm = min(512, seq)
    blk_a = min(256, seq)
    blk_bc = min(128, seq)
    bf16 = jnp.bfloat16

    cos, shi, slo = _rope_tables(seq)
    tri = (jnp.arange(tm)[:, None] >= jnp.arange(tm)[None, :]).astype(bf16)

    h = _layernorm(x.reshape(bsz * seq, d), ln_in_g, ln_in_b, tm).reshape(bsz, seq, d)
    for layer in range(depth):
        w = w_in[layer]
        o_b = 4 * wa
        o_f = o_b + 3 * wb
        o_g = o_f + hb
        o_c = o_g + wb
        w_a = w[:, :o_b].astype(bf16)
        w_b = jnp.concatenate([w[:, o_b:o_f], w[:, o_g:o_c]], axis=1).astype(bf16)
        w_f = jnp.pad(w[:, o_f:o_g], ((0, 0), (0, LANES - hb))).astype(bf16)
        w_c = w[:, o_c:].astype(bf16)
        b_f = jnp.pad(b_forget[layer], (0, LANES - hb)).reshape(1, LANES)

        (qa, ka, va, ga, qb, kb, vb, gb, c_col, qc, kc, vc, gc) = _inproj(
            h, w_a, w_b, w_f, w_c, b_f, cos, shi, slo, tri, tm)
        c_row = jnp.pad(jnp.swapaxes(c_col[:, :, :hb], 1, 2).reshape(bsz, hb // 2, 2, seq),
                        ((0, 0), (0, 0), (0, 6), (0, 0)))

        lam_init = 0.8 - 0.6 * math.exp(-0.3 * layer)
        lam_vecs = jnp.stack([lambda_q1[layer], lambda_k1[layer], lambda_q2[layer], lambda_k2[layer]])
        ya = _attn_a(lam_vecs, subln_g[layer].reshape(1, LANES), qa, ka, va, ga, lam_init, blk_a)
        yb = _attn_b(qb, kb, vb, gb, c_col, c_row, blk_bc)
        yc = _attn_c(qc, kc, vc, gc, blk_bc)

        m = bsz * seq
        h = _outproj(ya.reshape(m, wa), yb.reshape(m, wb), yc.reshape(m, wc), h.reshape(m, d),
                     w_out[layer].astype(bf16), ln_g[layer].reshape(1, d), ln_b[layer].reshape(1, d),
                     alpha, tm).reshape(bsz, seq, d)
    return h
```

```python
import functools
import math

import jax
import jax.numpy as jnp
from jax import lax
from jax.experimental import pallas as pl
from jax.experimental.pallas import tpu as pltpu

HEAD_DIM = 64
LANES = 128
ROT_DIM = HEAD_DIM // 4
ROPE_THETA = 500000.0
LN_EPS = 1e-5
SUBLN_EPS = 1e-5
LOG2E = 1.4426950408889634
NEG_BIG = -1e30
SKIP_NATS = 100.0
VMEM_LIMIT = 56 * 1024 * 1024
SOFTMAX_ROWS = 32

_NT = (((1,), (1,)), ((), ()))


def _dot(a, b):
    return jnp.dot(a, b, preferred_element_type=jnp.float32)


def _dot_nt(a, b):
    return lax.dot_general(a, b, _NT, preferred_element_type=jnp.float32)


def _layernorm_rows(t, g, b):
    mu = jnp.mean(t, axis=-1, keepdims=True)
    d = t - mu
    var = jnp.mean(d * d, axis=-1, keepdims=True)
    return d * lax.rsqrt(var + LN_EPS) * g + b


def _ln_kernel(x_ref, g_ref, b_ref, o_ref):
    o_ref[...] = _layernorm_rows(x_ref[...], g_ref[...], b_ref[...])


def _layernorm(x2d, g, b, tm):
    m, d = x2d.shape
    return pl.pallas_call(
        _ln_kernel,
        grid=(m // tm,),
        in_specs=[pl.BlockSpec((tm, d), lambda i: (i, 0)),
                  pl.BlockSpec((1, d), lambda i: (0, 0)),
                  pl.BlockSpec((1, d), lambda i: (0, 0))],
        out_specs=pl.BlockSpec((tm, d), lambda i: (i, 0)),
        out_shape=jax.ShapeDtypeStruct((m, d), jnp.float32),
        compiler_params=pltpu.CompilerParams(vmem_limit_bytes=VMEM_LIMIT),
        name="ln_in",
    )(x2d, g.reshape(1, d), b.reshape(1, d))


def _rope(t, cos, sin_hi, sin_lo):
    out = []
    for c in range(t.shape[1] // LANES):
        x = t[:, c * LANES:(c + 1) * LANES]
        out.append(x * cos
                   + pltpu.roll(x, ROT_DIM // 2, axis=1) * sin_hi
                   + pltpu.roll(x, LANES - ROT_DIM // 2, axis=1) * sin_lo)
    return jnp.concatenate(out, axis=1)


def _silu(t):
    return t * jax.nn.sigmoid(t)


def _log_sigmoid(t):
    return jnp.minimum(t, 0.0) - jnp.log1p(jnp.exp(-jnp.abs(t)))


def _inproj_kernel(h_ref, wa_ref, wb_ref, wf_ref, wc_ref, bf_ref, cos_ref, shi_ref, slo_ref, tri_ref,
                   qa_ref, ka_ref, va_ref, ga_ref, qb_ref, kb_ref, vb_ref, gb_ref, cb_ref, cr_ref,
                   qc_ref, kc_ref, vc_ref, gc_ref, carry_ref, *, wa, wb, wc):
    scale = HEAD_DIM ** -0.5
    hb = h_ref[0].astype(jnp.bfloat16)
    cos, shi, slo = cos_ref[...], shi_ref[...], slo_ref[...]
    bf16 = jnp.bfloat16

    qa_ref[0] = (_rope(_dot(hb, wa_ref[:, 0:wa]), cos, shi, slo) * (scale * LOG2E)).astype(bf16)
    ka_ref[0] = _rope(_dot(hb, wa_ref[:, wa:2 * wa]), cos, shi, slo).astype(bf16)
    va_ref[0] = _dot(hb, wa_ref[:, 2 * wa:3 * wa]).astype(bf16)
    ga_ref[0] = _silu(_dot(hb, wa_ref[:, 3 * wa:4 * wa])).astype(bf16)

    qb_ref[0] = (_dot(hb, wb_ref[:, 0:wb]) * (scale * LOG2E)).astype(bf16)
    kb_ref[0] = _dot(hb, wb_ref[:, wb:2 * wb]).astype(bf16)
    vb_ref[0] = _dot(hb, wb_ref[:, 2 * wb:3 * wb]).astype(bf16)
    gb_ref[0] = _silu(_dot(hb, wb_ref[:, 3 * wb:4 * wb])).astype(bf16)

    @pl.when(pl.program_id(1) == 0)
    def _():
        carry_ref[...] = jnp.zeros_like(carry_ref)

    logf = _log_sigmoid(_dot(hb, wf_ref[...]) + bf_ref[...])
    p1 = logf.astype(bf16)
    r1 = logf - p1.astype(jnp.float32)
    p2 = r1.astype(bf16)
    p3 = (r1 - p2.astype(jnp.float32)).astype(bf16)
    cs = _dot(tri_ref[...], jnp.concatenate([p1, p2, p3], axis=1))
    c = cs[:, 0:LANES] + cs[:, LANES:2 * LANES] + cs[:, 2 * LANES:3 * LANES] + carry_ref[...]
    carry_ref[...] = c[c.shape[0] - 1:, :]
    c2 = c * LOG2E
    cb_ref[0] = c2
    c2t = c2.T
    for p in range(cr_ref.shape[1]):
        cr_ref[0, p] = c2t[8 * p:8 * p + 8, :]

    qc_ref[0] = (_dot(hb, wc_ref[:, 0:wc]) * scale).astype(bf16)
    kc_ref[0] = _dot(hb, wc_ref[:, wc:2 * wc]).astype(bf16)
    vc_ref[0] = _dot(hb, wc_ref[:, 2 * wc:3 * wc]).astype(bf16)
    gc_ref[0] = _silu(_dot(hb, wc_ref[:, 3 * wc:4 * wc])).astype(bf16)


def _inproj(h, w_a, w_b, w_f, w_c, b_f, cos, shi, slo, tri, tm):
    bsz, seq, d = h.shape
    wa, wb, wc = w_a.shape[1] // 4, w_b.shape[1] // 4, w_c.shape[1] // 4
    row = lambda n: pl.BlockSpec((1, tm, n), lambda b, t: (b, t, 0))
    full = lambda a: pl.BlockSpec(a.shape, lambda b, t: (0,) * a.ndim)
    tab = pl.BlockSpec((tm, LANES), lambda b, t: (t, 0))
    bf16 = jnp.bfloat16
    shp = lambda n, dt=bf16: jax.ShapeDtypeStruct((bsz, seq, n), dt)
    pairs = wb // LANES
    c_row_spec = pl.BlockSpec((1, pairs, 8, tm), lambda b, t: (b, 0, 0, t))
    c_row_shape = jax.ShapeDtypeStruct((bsz, pairs, 8, seq), jnp.float32)
    return pl.pallas_call(
        functools.partial(_inproj_kernel, wa=wa, wb=wb, wc=wc),
        grid=(bsz, seq // tm),
        in_specs=[row(d), full(w_a), full(w_b), full(w_f), full(w_c), full(b_f), tab, tab, tab, full(tri)],
        out_specs=[row(wa)] * 4 + [row(wb)] * 4 + [row(LANES), c_row_spec] + [row(wc)] * 4,
        out_shape=([shp(wa)] * 4 + [shp(wb)] * 4 + [shp(LANES, jnp.float32), c_row_shape]
                   + [shp(wc)] * 4),
        scratch_shapes=[pltpu.VMEM((1, LANES), jnp.float32)],
        compiler_params=pltpu.CompilerParams(
            dimension_semantics=("arbitrary", "arbitrary"), vmem_limit_bytes=VMEM_LIMIT),
        name="in_proj",
    )(h, w_a, w_b, w_f, w_c, b_f, cos, shi, slo, tri)


def _head_masks(rows):
    lane = lax.broadcasted_iota(jnp.int32, (rows, LANES), 1)
    return lane < HEAD_DIM


def _causal(rows, cols, strict):
    r = lax.broadcasted_iota(jnp.int32, (rows, cols), 0)
    c = lax.broadcasted_iota(jnp.int32, (rows, cols), 1)
    return (c < r) if strict else (c <= r)


def _softmax_step(s, v, m_ref, l_ref, acc_ref, h):
    m_old = m_ref[h]
    m_new = jnp.maximum(m_old, jnp.max(s, axis=1, keepdims=True))
    alpha = jnp.exp2(m_old - m_new)
    p = jnp.exp2(s - m_new)
    l_ref[h] = alpha * l_ref[h] + jnp.sum(p, axis=1, keepdims=True)
    acc_ref[h] = alpha * acc_ref[h] + _dot(p.astype(jnp.bfloat16), v)
    m_ref[h] = m_new


def _stack_heads(q, low):
    zero = jnp.zeros_like(q)
    return jnp.concatenate([jnp.where(low, q, zero), jnp.where(low, zero, q)], axis=0)


def _lane_tile(t, width):
    return jnp.concatenate([t] * (width // LANES), axis=1)


def _lane_fold(t):
    out = t[:, 0:LANES]
    for c in range(1, t.shape[1] // LANES):
        out = out + t[:, c * LANES:(c + 1) * LANES]
    return out


def _attn_a_kernel(lam_ref, sg_ref, q_ref, k_ref, v_ref, g_ref, o_ref, m_ref, l_ref, acc_ref, alpha_ref,
                   s0_ref, s1_ref, p0_ref, p1_ref, *, blk, kw, lam_init):
    seq = q_ref.shape[1]
    low = _head_masks(blk)
    row_pos = lax.broadcasted_iota(jnp.int32, (SOFTMAX_ROWS, kw), 0)
    col_pos = lax.broadcasted_iota(jnp.int32, (SOFTMAX_ROWS, kw), 1)
    lv = lam_ref[...]
    lam = (jnp.exp(jnp.sum(lv[0:1] * lv[1:2], axis=1, keepdims=True))
           - jnp.exp(jnp.sum(lv[2:3] * lv[3:4], axis=1, keepdims=True)) + lam_init)

    def q_block(i, _):
        rows = pl.ds(pl.multiple_of(i * blk, blk), blk)
        q_st = _stack_heads(q_ref[0, rows, :], low)
        m_ref[...] = jnp.full_like(m_ref, NEG_BIG)
        l_ref[...] = jnp.zeros_like(l_ref)
        acc_ref[...] = jnp.zeros_like(acc_ref)
        n_full = (i * blk) // kw
        tail = jnp.maximum((i + 1) * blk - kw, 0)

        def cols(start):
            return pl.ds(pl.multiple_of(start, blk), kw)

        def scores(start):
            return _dot_nt(q_st, k_ref[0, cols(start), :])

        def softmax(s_ref, p_ref, masked):
            for r0 in range(0, 2 * blk, SOFTMAX_ROWS):
                rr = pl.ds(r0, SOFTMAX_ROWS)
                s = s_ref[rr, :]
                if masked:
                    key = tail + col_pos
                    keep = jnp.logical_and(key <= i * blk + (r0 % blk) + row_pos, key >= n_full * kw)
                    s = jnp.where(keep, s, NEG_BIG)
                m_old = m_ref[rr, :]
                m_new = jnp.maximum(m_old, jnp.max(s, axis=1, keepdims=True))
                alpha = jnp.exp2(m_old - m_new)
                alpha_ref[rr, :] = alpha
                p = jnp.exp2(s - _lane_tile(m_new, kw))
                l_ref[rr, :] = alpha * l_ref[rr, :] + _lane_fold(p)
                m_ref[rr, :] = m_new
                p_ref[rr, :] = p.astype(jnp.bfloat16)

        def stage(j, s_cur, s_nxt, p_prev, p_cur):
            pv = _dot(p_prev[...], v_ref[0, cols(jnp.maximum(j - 1, 0) * kw), :])
            if s_nxt is not None:
                s_nxt[...] = scores(jnp.minimum((j + 1) * kw, tail))
            softmax(s_cur, p_cur, masked=s_nxt is None)
            acc_ref[...] = alpha_ref[...] * (acc_ref[...] + pv)
            if s_nxt is None:
                acc_ref[...] += _dot(p_cur[...], v_ref[0, cols(tail), :])

        def by_parity(j, last):
            @pl.when(j % 2 == 0)
            def _():
                stage(j, s0_ref, None if last else s1_ref, p1_ref, p0_ref)

            @pl.when(j % 2 == 1)
            def _():
                stage(j, s1_ref, None if last else s0_ref, p0_ref, p1_ref)

        s0_ref[...] = scores(0)
        p1_ref[...] = jnp.zeros_like(p1_ref)
        lax.fori_loop(0, n_full, lambda j, c: (by_parity(j, False), c)[1], 0)
        by_parity(n_full, True)

        o_st = acc_ref[...] / jnp.sum(l_ref[...], axis=1, keepdims=True)
        o = o_st[0:blk] - lam * o_st[blk:2 * blk]
        o = o * lax.rsqrt(jnp.mean(o * o, axis=1, keepdims=True) + SUBLN_EPS)
        o = o * (sg_ref[...] * (1.0 - lam_init))
        o_ref[0, rows, :] = (o * g_ref[0, rows, :].astype(jnp.float32)).astype(o_ref.dtype)
        return 0

    lax.fori_loop(0, seq // blk, q_block, 0)


def _attn_a(lam_vecs, subln_g, qa, ka, va, ga, lam_init, blk, kw):
    bsz, seq, width = qa.shape
    heads = width // LANES
    assert seq % blk == 0 and kw % blk == 0 and seq >= kw
    spec = pl.BlockSpec((1, seq, LANES), lambda b, h: (b, 0, h))
    small = lambda a: pl.BlockSpec(a.shape, lambda b, h: (0, 0))
    return pl.pallas_call(
        functools.partial(_attn_a_kernel, blk=blk, kw=kw, lam_init=lam_init),
        grid=(bsz, heads),
        in_specs=[small(lam_vecs), small(subln_g), spec, spec, spec, spec],
        out_specs=spec,
        out_shape=jax.ShapeDtypeStruct((bsz, seq, width), jnp.bfloat16),
        scratch_shapes=([pltpu.VMEM((2 * blk, LANES), jnp.float32)] * 4
                        + [pltpu.VMEM((2 * blk, kw), jnp.float32)] * 2
                        + [pltpu.VMEM((2 * blk, kw), jnp.bfloat16)] * 2),
        compiler_params=pltpu.CompilerParams(
            dimension_semantics=("arbitrary", "arbitrary"), vmem_limit_bytes=VMEM_LIMIT),
        name="attn_diff",
    )(lam_vecs, subln_g, qa, ka, va, ga)


def _attn_b_kernel(q_ref, k_ref, v_ref, g_ref, cc_ref, cr_ref, o_ref, m_ref, l_ref, acc_ref, kn_ref,
                   *, blk):
    seq = q_ref.shape[1]
    pair = pl.program_id(1)
    low = _head_masks(blk)
    diag = _causal(blk, blk, strict=False)
    lane = lax.broadcasted_iota(jnp.int32, (1, LANES), 1)
    f32 = jnp.float32

    sel = (lax.broadcasted_iota(jnp.int32, (LANES, LANES), 0) // HEAD_DIM
           == lax.broadcasted_iota(jnp.int32, (LANES, LANES), 1)).astype(jnp.bfloat16)
    kn_ref[...] = jnp.zeros_like(kn_ref)

    def kn_block(j, _):
        kk = k_ref[0, pl.ds(pl.multiple_of(j * blk, blk), blk), :].astype(f32)
        n2 = _dot((kk * kk).astype(jnp.bfloat16), sel)
        kn_ref[...] = jnp.maximum(kn_ref[...], jnp.max(n2, axis=0, keepdims=True))
        return 0

    lax.fori_loop(0, seq // blk, kn_block, 0)
    kn = jnp.sqrt(kn_ref[...]) * 1.01

    def q_block(i, _):
        rows = pl.ds(pl.multiple_of(i * blk, blk), blk)
        q = q_ref[0, rows, :]
        zero = jnp.zeros_like(q)
        qs = (jnp.where(low, q, zero), jnp.where(low, zero, q))
        qf = q.astype(f32)
        qn2 = _dot((qf * qf).astype(jnp.bfloat16), sel)
        cc = cc_ref[0, rows, :]
        reach, cq = [], []
        for h in range(2):
            hsel = lane == h
            qn_h = jnp.sqrt(jnp.sum(jnp.where(hsel, qn2, 0.0), axis=1, keepdims=True)) * 1.01
            kn_h = jnp.sum(jnp.where(hsel, kn, 0.0), axis=1, keepdims=True)
            cq_h = jnp.sum(jnp.where(lane == 8 * pair + h, cc, 0.0), axis=1, keepdims=True)
            reach.append(qn_h * kn_h + cq_h)
            cq.append(cq_h)
        m_ref[...] = jnp.full_like(m_ref, NEG_BIG)
        l_ref[...] = jnp.zeros_like(l_ref)
        acc_ref[...] = jnp.zeros_like(acc_ref)

        def step(j, masked):
            cols = pl.ds(pl.multiple_of(j * blk, blk), blk)
            k = k_ref[0, cols, :]
            v = v_ref[0, cols, :]
            alive = jnp.zeros((1, 1), f32)
            for h in range(2):
                ck = cr_ref[0, 0, h:h + 1, cols]
                s = _dot_nt(qs[h], k) + (cq[h] - ck)
                if masked:
                    s = jnp.where(diag, s, NEG_BIG)
                _softmax_step(s, v, m_ref, l_ref, acc_ref, h)
                bound = reach[h] - jnp.max(ck, axis=1, keepdims=True)
                live = jnp.max(bound - m_ref[h], axis=0, keepdims=True) >= -SKIP_NATS * LOG2E
                alive = jnp.maximum(alive, live.astype(f32))
            return jnp.max(alive) > 0.0

        go = step(i, True)

        def cond(c):
            return jnp.logical_and(c[0] >= 0, c[1])

        def body(c):
            return c[0] - 1, step(c[0], False)

        lax.while_loop(cond, body, (i - 1, go))

        o = jnp.where(low, acc_ref[0] / l_ref[0], acc_ref[1] / l_ref[1])
        o_ref[0, rows, :] = (o * g_ref[0, rows, :].astype(f32)).astype(o_ref.dtype)
        return 0

    lax.fori_loop(0, seq // blk, q_block, 0)


def _attn_b(qb, kb, vb, gb, c_col, c_row, blk):
    bsz, seq, width = qb.shape
    pairs = width // LANES
    spec = pl.BlockSpec((1, seq, LANES), lambda b, h: (b, 0, h))
    return pl.pallas_call(
        functools.partial(_attn_b_kernel, blk=blk),
        grid=(bsz, pairs),
        in_specs=[spec, spec, spec, spec,
                  pl.BlockSpec((1, seq, LANES), lambda b, h: (b, 0, 0)),
                  pl.BlockSpec((1, 1) + c_row.shape[2:], lambda b, h: (b, h, 0, 0))],
        out_specs=spec,
        out_shape=jax.ShapeDtypeStruct((bsz, seq, width), jnp.bfloat16),
        scratch_shapes=[pltpu.VMEM((2, blk, 1), jnp.float32), pltpu.VMEM((2, blk, 1), jnp.float32),
                        pltpu.VMEM((2, blk, LANES), jnp.float32), pltpu.VMEM((1, LANES), jnp.float32)],
        compiler_params=pltpu.CompilerParams(
            dimension_semantics=("arbitrary", "arbitrary"), vmem_limit_bytes=VMEM_LIMIT),
        name="attn_forget",
    )(qb, kb, vb, gb, c_col, c_row)


def _attn_c_kernel(q_ref, k_ref, v_ref, g_ref, o_ref, run_ref, acc_ref, *, blk):
    seq = q_ref.shape[1]
    low = _head_masks(blk)
    strict = _causal(blk, blk, strict=True)
    f32, bf16 = jnp.float32, jnp.bfloat16
    jj = lax.broadcasted_iota(jnp.int32, (2 * blk, blk), 0) % blk
    ss = lax.broadcasted_iota(jnp.int32, (2 * blk, blk), 1)
    after_op = (jj > ss).astype(bf16)

    def q_block(i, _):
        rows = pl.ds(pl.multiple_of(i * blk, blk), blk)
        q = q_ref[0, rows, :]
        zero = jnp.zeros_like(q)
        qs = (jnp.where(low, q, zero), jnp.where(low, zero, q))
        run_ref[...] = jnp.zeros_like(run_ref)
        acc_ref[...] = jnp.zeros_like(acc_ref)

        def step(j, masked):
            cols = pl.ds(pl.multiple_of(j * blk, blk), blk)
            k = k_ref[0, cols, :]
            v = v_ref[0, cols, :]
            alive = jnp.zeros((1, 1), f32)
            for h in range(2):
                z = _dot_nt(qs[h], k)
                log_fail = -(jnp.maximum(z, 0.0) + jnp.log1p(jnp.exp(-jnp.abs(z))))
                log_hit = z + log_fail
                if masked:
                    log_fail = jnp.where(strict, log_fail, 0.0)
                hi = log_fail.astype(bf16)
                lo = (log_fail - hi.astype(f32)).astype(bf16)
                after = _dot(jnp.concatenate([hi, lo], axis=1), after_op) + run_ref[h]
                w = jnp.exp(log_hit + after)
                if masked:
                    w = jnp.where(strict, w, 0.0)
                acc_ref[h] = acc_ref[h] + _dot(w.astype(bf16), v)
                run = run_ref[h] + jnp.sum(log_fail, axis=1, keepdims=True)
                run_ref[h] = run
                live = jnp.max(run, axis=0, keepdims=True) >= -SKIP_NATS
                alive = jnp.maximum(alive, live.astype(f32))
            return jnp.max(alive) > 0.0

        go = step(i, True)

        def cond(c):
            return jnp.logical_and(c[0] >= 0, c[1])

        def body(c):
            return c[0] - 1, step(c[0], False)

        lax.while_loop(cond, body, (i - 1, go))

        o = jnp.where(low, acc_ref[0], acc_ref[1])
        o_ref[0, rows, :] = (o * g_ref[0, rows, :].astype(f32)).astype(o_ref.dtype)
        return 0

    lax.fori_loop(0, seq // blk, q_block, 0)


def _attn_c(qc, kc, vc, gc, blk):
    bsz, seq, width = qc.shape
    pairs = width // LANES
    spec = pl.BlockSpec((1, seq, LANES), lambda b, h: (b, 0, h))
    return pl.pallas_call(
        functools.partial(_attn_c_kernel, blk=blk),
        grid=(bsz, pairs),
        in_specs=[spec, spec, spec, spec],
        out_specs=spec,
        out_shape=jax.ShapeDtypeStruct((bsz, seq, width), jnp.bfloat16),
        scratch_shapes=[pltpu.VMEM((2, blk, 1), jnp.float32), pltpu.VMEM((2, blk, LANES), jnp.float32)],
        compiler_params=pltpu.CompilerParams(
            dimension_semantics=("arbitrary", "arbitrary"), vmem_limit_bytes=VMEM_LIMIT),
        name="attn_stick",
    )(qc, kc, vc, gc)


def _outproj_kernel(ya_ref, yb_ref, yc_ref, h_ref, w_ref, g_ref, b_ref, o_ref, *, alpha):
    wa, wb = ya_ref.shape[1], yb_ref.shape[1]
    y = (_dot(ya_ref[...], w_ref[0:wa, :]) + _dot(yb_ref[...], w_ref[wa:wa + wb, :])
         + _dot(yc_ref[...], w_ref[wa + wb:, :]))
    o_ref[...] = _layernorm_rows(alpha * h_ref[...] + y, g_ref[...], b_ref[...])


def _outproj(ya, yb, yc, h2d, w, g, b, alpha, tm):
    m, d = h2d.shape
    row = lambda a: pl.BlockSpec((tm, a.shape[1]), lambda i: (i, 0))
    full = lambda a: pl.BlockSpec(a.shape, lambda i: (0, 0))
    return pl.pallas_call(
        functools.partial(_outproj_kernel, alpha=alpha),
        grid=(m // tm,),
        in_specs=[row(ya), row(yb), row(yc), row(h2d), full(w), full(g), full(b)],
        out_specs=pl.BlockSpec((tm, d), lambda i: (i, 0)),
        out_shape=jax.ShapeDtypeStruct((m, d), jnp.float32),
        compiler_params=pltpu.CompilerParams(vmem_limit_bytes=VMEM_LIMIT),
        name="out_proj_ln",
    )(ya, yb, yc, h2d, w, g, b)


def _rope_tables(seq):
    half = ROT_DIM // 2
    inv = ROPE_THETA ** (-jnp.arange(0, ROT_DIM, 2, dtype=jnp.float32) / ROT_DIM)
    ang = jnp.arange(seq, dtype=jnp.float32)[:, None] * inv[None, :]
    cos, sin = jnp.cos(ang), jnp.sin(ang)
    pad = HEAD_DIM - ROT_DIM
    one_head = lambda a, b, fill: jnp.concatenate(
        [a, b, jnp.full((seq, pad), fill, jnp.float32)], axis=1)
    zeros = jnp.zeros((seq, half), jnp.float32)
    reps = LANES // HEAD_DIM
    cos_t = jnp.tile(one_head(cos, cos, 1.0), (1, reps))
    sin_hi = jnp.tile(one_head(zeros, sin, 0.0), (1, reps))
    sin_lo = jnp.tile(one_head(-sin, zeros, 0.0), (1, reps))
    return cos_t, sin_hi, sin_lo


def kernel(x, ln_in_g, ln_in_b, w_in, b_forget, lambda_q1, lambda_k1, lambda_q2, lambda_k2,
           subln_g, w_out, ln_g, ln_b):
    bsz, seq, d = x.shape
    depth = w_in.shape[0]
    hb = b_forget.shape[1]
    d_mix = w_out.shape[1]
    wa, wb, wc = d_mix // 2, d_mix // 4, d_mix // 4
    assert w_in.shape[2] == 4 * wa + 4 * wb + hb + 4 * wc and wb == hb * HEAD_DIM
    alpha = (2 * depth) ** 0.25
    tm = min(512, seq)
    blk_a = min(512, seq)
    kw_a = min(512, seq)
    blk_bc = min(128, seq)
    bf16 = jnp.bfloat16

    cos, shi, slo = _rope_tables(seq)
    tri = (jnp.arange(tm)[:, None] >= jnp.arange(tm)[None, :]).astype(bf16)

    h = _layernorm(x.reshape(bsz * seq, d), ln_in_g, ln_in_b, tm).reshape(bsz, seq, d)
    for layer in range(depth):
        w = w_in[layer]
        o_b = 4 * wa
        o_f = o_b + 3 * wb
        o_g = o_f + hb
        o_c = o_g + wb
        w_a = w[:, :o_b].astype(bf16)
        w_b = jnp.concatenate([w[:, o_b:o_f], w[:, o_g:o_c]], axis=1).astype(bf16)
        spread = lambda a: jnp.pad(a.reshape(a.shape[:-1] + (hb // 2, 2)),
                                   [(0, 0)] * (a.ndim - 1) + [(0, LANES // 8 - hb // 2), (0, 6)]
                                   ).reshape(a.shape[:-1] + (LANES,))
        w_f = spread(w[:, o_f:o_g]).astype(bf16)
        w_c = w[:, o_c:].astype(bf16)
        b_f = spread(b_forget[layer]).reshape(1, LANES)

        (qa, ka, va, ga, qb, kb, vb, gb, c_col, c_row, qc, kc, vc, gc) = _inproj(
            h, w_a, w_b, w_f, w_c, b_f, cos, shi, slo, tri, tm)

        lam_init = 0.8 - 0.6 * math.exp(-0.3 * layer)
        lam_vecs = jnp.stack([lambda_q1[layer], lambda_k1[layer], lambda_q2[layer], lambda_k2[layer]])
        ya = _attn_a(lam_vecs, subln_g[layer].reshape(1, LANES), qa, ka, va, ga, lam_init, blk_a, kw_a)
        yb = _attn_b(qb, kb, vb, gb, c_col, c_row, blk_bc)
        yc = _attn_c(qc, kc, vc, gc, blk_bc)

        m = bsz * seq
        h = _outproj(ya.reshape(m, wa), yb.reshape(m, wb), yc.reshape(m, wc), h.reshape(m, d),
                     w_out[layer].astype(bf16), ln_g[layer].reshape(1, d), ln_b[layer].reshape(1, d),
                     alpha, tm).reshape(bsz, seq, d)
    return h
```

```python
import functools
import math

import jax
import jax.numpy as jnp
from jax import lax
from jax.experimental import pallas as pl
from jax.experimental.pallas import tpu as pltpu

HEAD_DIM = 64
LANES = 128
SUBLANES = 8
ROT_DIM = HEAD_DIM // 4
ROPE_THETA = 500000.0
LN_EPS = 1e-5
SUBLN_EPS = 1e-5
LOG2E = 1.4426950408889634
NEG_BIG = -1e30
SKIP_NATS = 100.0
VMEM_LIMIT = 56 * 1024 * 1024
SOFTMAX_ROWS = 32

_NT = (((1,), (1,)), ((), ()))


def _dot(a, b):
    return jnp.dot(a, b, preferred_element_type=jnp.float32)


def _dot_nt(a, b):
    return lax.dot_general(a, b, _NT, preferred_element_type=jnp.float32)


def _layernorm_rows(t, g, b):
    mu = jnp.mean(t, axis=-1, keepdims=True)
    d = t - mu
    var = jnp.mean(d * d, axis=-1, keepdims=True)
    return d * lax.rsqrt(var + LN_EPS) * g + b


def _ln_kernel(x_ref, g_ref, b_ref, o_ref):
    o_ref[...] = _layernorm_rows(x_ref[...], g_ref[...], b_ref[...])


def _layernorm(x2d, g, b, tm):
    m, d = x2d.shape
    return pl.pallas_call(
        _ln_kernel,
        grid=(m // tm,),
        in_specs=[pl.BlockSpec((tm, d), lambda i: (i, 0)),
                  pl.BlockSpec((1, d), lambda i: (0, 0)),
                  pl.BlockSpec((1, d), lambda i: (0, 0))],
        out_specs=pl.BlockSpec((tm, d), lambda i: (i, 0)),
        out_shape=jax.ShapeDtypeStruct((m, d), jnp.float32),
        compiler_params=pltpu.CompilerParams(vmem_limit_bytes=VMEM_LIMIT),
        name="ln_in",
    )(x2d, g.reshape(1, d), b.reshape(1, d))


def _cast_kernel(*refs):
    n = len(refs) // 2
    for src, dst in zip(refs[:n], refs[n:]):
        dst[...] = src[...].astype(dst.dtype)


def _cast_bf16(arrays, tr):
    rows = arrays[0].shape[0]
    spec = lambda a: pl.BlockSpec((tr, a.shape[1]), lambda i: (i, 0))
    return pl.pallas_call(
        _cast_kernel,
        grid=(rows // tr,),
        in_specs=[spec(a) for a in arrays],
        out_specs=[spec(a) for a in arrays],
        out_shape=[jax.ShapeDtypeStruct(a.shape, jnp.bfloat16) for a in arrays],
        compiler_params=pltpu.CompilerParams(vmem_limit_bytes=VMEM_LIMIT),
        name="cast_weights",
    )(*arrays)


def _rope(t, cos, sin_hi, sin_lo):
    out = []
    for c in range(t.shape[1] // LANES):
        x = t[:, c * LANES:(c + 1) * LANES]
        out.append(x * cos
                   + pltpu.roll(x, ROT_DIM // 2, axis=1) * sin_hi
                   + pltpu.roll(x, LANES - ROT_DIM // 2, axis=1) * sin_lo)
    return jnp.concatenate(out, axis=1)


def _silu(t):
    return t * jax.nn.sigmoid(t)


def _softplus(t):
    return jnp.maximum(t, 0.0) + jnp.log1p(jnp.exp(-jnp.abs(t)))


def _inproj_kernel(h_ref, wa_ref, wb_ref, wf_ref, wc_ref, bf_ref, cos_ref, shi_ref, slo_ref, tri_ref,
                   qa_ref, ka_ref, va_ref, ga_ref, qb_ref, kb_ref, vb_ref, gb_ref, cb_ref, cr_ref,
                   qc_ref, kc_ref, vc_ref, gc_ref, carry_ref, *, wa, wb, wc):
    scale = HEAD_DIM ** -0.5
    hb = h_ref[0].astype(jnp.bfloat16)
    cos, shi, slo = cos_ref[...], shi_ref[...], slo_ref[...]
    bf16 = jnp.bfloat16

    qa_ref[0] = (_rope(_dot(hb, wa_ref[:, 0:wa]), cos, shi, slo) * (scale * LOG2E)).astype(bf16)
    ka_ref[0] = _rope(_dot(hb, wa_ref[:, wa:2 * wa]), cos, shi, slo).astype(bf16)
    va_ref[0] = _dot(hb, wa_ref[:, 2 * wa:3 * wa]).astype(bf16)
    ga_ref[0] = _silu(_dot(hb, wa_ref[:, 3 * wa:4 * wa])).astype(bf16)

    qb_ref[0] = (_dot(hb, wb_ref[:, 0:wb]) * (scale * LOG2E)).astype(bf16)
    kb_ref[0] = _dot(hb, wb_ref[:, wb:2 * wb]).astype(bf16)
    vb_ref[0] = _dot(hb, wb_ref[:, 2 * wb:3 * wb]).astype(bf16)
    gb_ref[0] = _silu(_dot(hb, wb_ref[:, 3 * wb:4 * wb])).astype(bf16)

    @pl.when(pl.program_id(1) == 0)
    def _():
        carry_ref[...] = jnp.zeros_like(carry_ref)

    logf = -_softplus(-(_dot(hb, wf_ref[...]) + bf_ref[...]))
    p1 = logf.astype(bf16)
    r1 = logf - p1.astype(jnp.float32)
    p2 = r1.astype(bf16)
    p3 = (r1 - p2.astype(jnp.float32)).astype(bf16)
    cs = _dot(tri_ref[...], jnp.concatenate([p1, p2, p3], axis=1))
    c = cs[:, 0:LANES] + cs[:, LANES:2 * LANES] + cs[:, 2 * LANES:3 * LANES] + carry_ref[...]
    carry_ref[...] = c[c.shape[0] - 1:, :]
    c2 = c * LOG2E
    cb_ref[0] = c2
    c2t = c2.T
    for p in range(cr_ref.shape[1]):
        cr_ref[0, p] = c2t[SUBLANES * p:SUBLANES * (p + 1), :]

    qc_ref[0] = (_dot(hb, wc_ref[:, 0:wc]) * scale).astype(bf16)
    kc_ref[0] = _dot(hb, wc_ref[:, wc:2 * wc]).astype(bf16)
    vc_ref[0] = _dot(hb, wc_ref[:, 2 * wc:3 * wc]).astype(bf16)
    gc_ref[0] = _silu(_dot(hb, wc_ref[:, 3 * wc:4 * wc])).astype(bf16)


def _inproj(h, w_a, w_b, w_f, w_c, b_f, cos, shi, slo, tri, tm):
    bsz, seq, d = h.shape
    wa, wb, wc = w_a.shape[1] // 4, w_b.shape[1] // 4, w_c.shape[1] // 4
    row = lambda n: pl.BlockSpec((1, tm, n), lambda b, t: (b, t, 0))
    full = lambda a: pl.BlockSpec(a.shape, lambda b, t: (0,) * a.ndim)
    tab = pl.BlockSpec((tm, LANES), lambda b, t: (t, 0))
    bf16 = jnp.bfloat16
    shp = lambda n, dt=bf16: jax.ShapeDtypeStruct((bsz, seq, n), dt)
    pairs = wb // LANES
    c_row_spec = pl.BlockSpec((1, pairs, SUBLANES, tm), lambda b, t: (b, 0, 0, t))
    c_row_shape = jax.ShapeDtypeStruct((bsz, pairs, SUBLANES, seq), jnp.float32)
    return pl.pallas_call(
        functools.partial(_inproj_kernel, wa=wa, wb=wb, wc=wc),
        grid=(bsz, seq // tm),
        in_specs=[row(d), full(w_a), full(w_b), full(w_f), full(w_c), full(b_f), tab, tab, tab, full(tri)],
        out_specs=[row(wa)] * 4 + [row(wb)] * 4 + [row(LANES), c_row_spec] + [row(wc)] * 4,
        out_shape=([shp(wa)] * 4 + [shp(wb)] * 4 + [shp(LANES, jnp.float32), c_row_shape]
                   + [shp(wc)] * 4),
        scratch_shapes=[pltpu.VMEM((1, LANES), jnp.float32)],
        compiler_params=pltpu.CompilerParams(
            dimension_semantics=("arbitrary", "arbitrary"), vmem_limit_bytes=VMEM_LIMIT),
        name="in_proj",
    )(h, w_a, w_b, w_f, w_c, b_f, cos, shi, slo, tri)


def _head_masks(rows):
    lane = lax.broadcasted_iota(jnp.int32, (rows, LANES), 1)
    return lane < HEAD_DIM


def _stack_heads(q, low):
    zero = jnp.zeros_like(q)
    return jnp.concatenate([jnp.where(low, q, zero), jnp.where(low, zero, q)], axis=0)


def _lane_tile(t, width):
    return jnp.concatenate([t] * (width // LANES), axis=1)


def _lane_fold(t):
    out = t[:, 0:LANES]
    for c in range(1, t.shape[1] // LANES):
        out = out + t[:, c * LANES:(c + 1) * LANES]
    return out


def _flash_scratch(blk, kw):
    return ([pltpu.VMEM((2 * blk, LANES), jnp.float32)] * 4
            + [pltpu.VMEM((2 * blk, kw), jnp.float32)] * 2
            + [pltpu.VMEM((2 * blk, kw), jnp.bfloat16)] * 2)


def _causal_flash(i, j_start, q_st, k_ref, v_ref, scratch, *, blk, kw, bias=None):
    m_ref, l_ref, acc_ref, alpha_ref, s0_ref, s1_ref, p0_ref, p1_ref = scratch
    s_refs, p_refs = (s0_ref, s1_ref), (p0_ref, p1_ref)
    row_pos = lax.broadcasted_iota(jnp.int32, (SOFTMAX_ROWS, kw), 0)
    col_pos = lax.broadcasted_iota(jnp.int32, (SOFTMAX_ROWS, kw), 1)
    n_full = (i * blk) // kw
    tail = jnp.maximum((i + 1) * blk - kw, 0)
    m_ref[...] = jnp.full_like(m_ref, NEG_BIG)
    l_ref[...] = jnp.zeros_like(l_ref)
    acc_ref[...] = jnp.zeros_like(acc_ref)

    def cols(start):
        return pl.ds(pl.multiple_of(start, LANES), kw)

    def scores(start):
        return _dot_nt(q_st, k_ref[0, cols(start), :])

    def softmax(s_ref, p_ref, start, masked):
        for r0 in range(0, 2 * blk, SOFTMAX_ROWS):
            rr = pl.ds(r0, SOFTMAX_ROWS)
            s = s_ref[rr, :]
            if bias is not None:
                s = s + bias(r0, start)
            if masked:
                key = start + col_pos
                keep = jnp.logical_and(key <= i * blk + (r0 % blk) + row_pos, key >= n_full * kw)
                s = jnp.where(keep, s, NEG_BIG)
            m_old = m_ref[rr, :]
            m_new = jnp.maximum(m_old, jnp.max(s, axis=1, keepdims=True))
            alpha = jnp.exp2(m_old - m_new)
            alpha_ref[rr, :] = alpha
            p = jnp.exp2(s - _lane_tile(m_new, kw))
            l_ref[rr, :] = alpha * l_ref[rr, :] + _lane_fold(p)
            m_ref[rr, :] = m_new
            p_ref[rr, :] = p.astype(jnp.bfloat16)

    def stage(j, par, last):
        pv = _dot(p_refs[1 - par][...], v_ref[0, cols(jnp.maximum(j - 1, 0) * kw), :])
        if not last:
            s_refs[1 - par][...] = scores(jnp.minimum((j + 1) * kw, tail))
        softmax(s_refs[par], p_refs[par], tail if last else j * kw, masked=last)
        acc_ref[...] = alpha_ref[...] * (acc_ref[...] + pv)
        if last:
            acc_ref[...] += _dot(p_refs[par][...], v_ref[0, cols(tail), :])

    def by_parity(j, fn):
        for par in range(2):
            pl.when(j % 2 == par)(functools.partial(fn, par))

    def prologue(par):
        s_refs[par][...] = scores(jnp.minimum(j_start * kw, tail))
        p_refs[1 - par][...] = jnp.zeros_like(p_refs[1 - par])

    if isinstance(j_start, int):
        prologue(j_start % 2)
    else:
        by_parity(j_start, prologue)

    def body(j, carry):
        by_parity(j, lambda par: stage(j, par, False))
        return carry

    lax.fori_loop(j_start, n_full, body, 0)
    by_parity(n_full, lambda par: stage(n_full, par, True))


def _attn_a_kernel(lam_ref, sg_ref, q_ref, k_ref, v_ref, g_ref, o_ref, *scratch, blk, kw, lam_init):
    seq = q_ref.shape[1]
    low = _head_masks(blk)
    l_ref, acc_ref = scratch[1], scratch[2]
    lv = lam_ref[...]
    lam = (jnp.exp(jnp.sum(lv[0:1] * lv[1:2], axis=1, keepdims=True))
           - jnp.exp(jnp.sum(lv[2:3] * lv[3:4], axis=1, keepdims=True)) + lam_init)

    def q_block(i, _):
        rows = pl.ds(pl.multiple_of(i * blk, blk), blk)
        q_st = _stack_heads(q_ref[0, rows, :], low)
        _causal_flash(i, 0, q_st, k_ref, v_ref, scratch, blk=blk, kw=kw)
        o_st = acc_ref[...] / jnp.sum(l_ref[...], axis=1, keepdims=True)
        o = o_st[0:blk] - lam * o_st[blk:2 * blk]
        o = o * lax.rsqrt(jnp.mean(o * o, axis=1, keepdims=True) + SUBLN_EPS)
        o = o * (sg_ref[...] * (1.0 - lam_init))
        o_ref[0, rows, :] = (o * g_ref[0, rows, :].astype(jnp.float32)).astype(o_ref.dtype)
        return 0

    lax.fori_loop(0, seq // blk, q_block, 0)


def _attn_a(lam_vecs, subln_g, qa, ka, va, ga, lam_init, blk, kw):
    bsz, seq, width = qa.shape
    heads = width // LANES
    assert seq % blk == 0 and kw % blk == 0 and seq >= kw
    spec = pl.BlockSpec((1, seq, LANES), lambda b, h: (b, 0, h))
    small = lambda a: pl.BlockSpec(a.shape, lambda b, h: (0, 0))
    return pl.pallas_call(
        functools.partial(_attn_a_kernel, blk=blk, kw=kw, lam_init=lam_init),
        grid=(bsz, heads),
        in_specs=[small(lam_vecs), small(subln_g), spec, spec, spec, spec],
        out_specs=spec,
        out_shape=jax.ShapeDtypeStruct((bsz, seq, width), jnp.bfloat16),
        scratch_shapes=_flash_scratch(blk, kw),
        compiler_params=pltpu.CompilerParams(
            dimension_semantics=("arbitrary", "arbitrary"), vmem_limit_bytes=VMEM_LIMIT),
        name="attn_diff",
    )(lam_vecs, subln_g, qa, ka, va, ga)


def _attn_b_kernel(q_ref, k_ref, v_ref, g_ref, cc_ref, cr_ref, ce_ref, o_ref, cq_ref, *scratch, blk, kw):
    seq = q_ref.shape[1]
    pair = pl.program_id(1)
    low = _head_masks(blk)
    lane = lax.broadcasted_iota(jnp.int32, (1, LANES), 1)
    l_ref, acc_ref = scratch[1], scratch[2]
    f32 = jnp.float32

    def kn_block(j, kn2):
        kk = k_ref[0, pl.ds(pl.multiple_of(j * blk, blk), blk), :].astype(f32)
        sq = kk * kk
        zero = jnp.zeros_like(sq)
        n_lo = jnp.max(jnp.sum(jnp.where(low, sq, zero), axis=1, keepdims=True), axis=0, keepdims=True)
        n_hi = jnp.max(jnp.sum(jnp.where(low, zero, sq), axis=1, keepdims=True), axis=0, keepdims=True)
        return jnp.maximum(kn2[0], n_lo), jnp.maximum(kn2[1], n_hi)

    kn2 = lax.fori_loop(0, seq // blk, kn_block, (jnp.zeros((1, 1), f32), jnp.zeros((1, 1), f32)))
    kn = (jnp.sqrt(kn2[0]), jnp.sqrt(kn2[1]))

    def q_block(i, _):
        rows = pl.ds(pl.multiple_of(i * blk, blk), blk)
        q_st = _stack_heads(q_ref[0, rows, :], low)
        qf = q_st.astype(f32)
        kf = k_ref[0, rows, :].astype(f32)
        qn = jnp.sqrt(jnp.sum(qf * qf, axis=1, keepdims=True))
        diag = jnp.sum(qf * jnp.concatenate([kf, kf], axis=0), axis=1, keepdims=True)
        cc = cc_ref[0, rows, :]
        cq = jnp.concatenate(
            [jnp.sum(jnp.where(lane == SUBLANES * pair + h, cc, 0.0), axis=1, keepdims=True)
             for h in range(2)], axis=0)
        cq_ref[...] = jnp.broadcast_to(cq, cq_ref.shape)

        n_full = (i * blk) // kw
        j_start = n_full
        for h in range(2):
            hr = slice(h * blk, (h + 1) * blk)
            reach = jnp.max(qn[hr] * kn[h] + cq[hr] - diag[hr], axis=0, keepdims=True)
            skipped = (ce_ref[0, 0, h:h + 1, :] > reach + SKIP_NATS * LOG2E).astype(jnp.int32)
            j_start = jnp.minimum(j_start, jnp.sum(skipped))

        def bias(r0, start):
            h = r0 // blk
            ck = cr_ref[0, 0, h:h + 1, pl.ds(pl.multiple_of(start, LANES), kw)]
            return _lane_tile(cq_ref[pl.ds(r0, SOFTMAX_ROWS), :], kw) - ck

        _causal_flash(i, j_start, q_st, k_ref, v_ref, scratch, blk=blk, kw=kw, bias=bias)
        o_st = acc_ref[...] / jnp.sum(l_ref[...], axis=1, keepdims=True)
        o = jnp.where(low, o_st[0:blk], o_st[blk:2 * blk])
        o_ref[0, rows, :] = (o * g_ref[0, rows, :].astype(f32)).astype(o_ref.dtype)
        return 0

    lax.fori_loop(0, seq // blk, q_block, 0)


def _attn_b(qb, kb, vb, gb, c_col, c_row, c_end, blk, kw):
    bsz, seq, width = qb.shape
    pairs = width // LANES
    assert seq % blk == 0 and kw % blk == 0 and seq >= kw
    spec = pl.BlockSpec((1, seq, LANES), lambda b, h: (b, 0, h))
    per_pair = lambda a: pl.BlockSpec((1, 1) + a.shape[2:], lambda b, h: (b, h, 0, 0))
    return pl.pallas_call(
        functools.partial(_attn_b_kernel, blk=blk, kw=kw),
        grid=(bsz, pairs),
        in_specs=[spec, spec, spec, spec,
                  pl.BlockSpec((1, seq, LANES), lambda b, h: (b, 0, 0)),
                  per_pair(c_row), per_pair(c_end)],
        out_specs=spec,
        out_shape=jax.ShapeDtypeStruct((bsz, seq, width), jnp.bfloat16),
        scratch_shapes=[pltpu.VMEM((2 * blk, LANES), jnp.float32)] + _flash_scratch(blk, kw),
        compiler_params=pltpu.CompilerParams(
            dimension_semantics=("arbitrary", "arbitrary"), vmem_limit_bytes=VMEM_LIMIT),
        name="attn_forget",
    )(qb, kb, vb, gb, c_col, c_row, c_end)


def _attn_c_kernel(q_ref, k_ref, v_ref, g_ref, o_ref, run_ref, acc_ref, z_ref, hl_ref, p_ref, *, blk):
    seq = q_ref.shape[1]
    nsub = blk // LANES
    low = _head_masks(blk)
    f32, bf16 = jnp.float32, jnp.bfloat16
    jj = lax.broadcasted_iota(jnp.int32, (2 * LANES, LANES), 0) % LANES
    ss = lax.broadcasted_iota(jnp.int32, (2 * LANES, LANES), 1)
    after_op = (jj > ss).astype(bf16)
    row_pos = lax.broadcasted_iota(jnp.int32, (SOFTMAX_ROWS, blk), 0)
    col_pos = lax.broadcasted_iota(jnp.int32, (SOFTMAX_ROWS, blk), 1)
    row_all = lax.broadcasted_iota(jnp.int32, (2 * blk, LANES), 0) % blk
    lane_all = lax.broadcasted_iota(jnp.int32, (2 * blk, LANES), 1)

    def q_block(i, _):
        rows = pl.ds(pl.multiple_of(i * blk, blk), blk)
        q_st = _stack_heads(q_ref[0, rows, :], low)
        run_ref[...] = jnp.zeros_like(run_ref)
        acc_ref[...] = jnp.zeros_like(acc_ref)

        def step(j, masked):
            cols = pl.ds(pl.multiple_of(j * blk, blk), blk)
            z_ref[...] = _dot_nt(q_st, k_ref[0, cols, :])
            for r0 in range(0, 2 * blk, SOFTMAX_ROWS):
                rr = pl.ds(r0, SOFTMAX_ROWS)
                z = z_ref[rr, :]
                sp = _softplus(z)
                log_fail = -sp
                log_hit = z - sp
                if masked:
                    log_fail = jnp.where(col_pos < (r0 % blk) + row_pos, log_fail, 0.0)
                hi = log_fail.astype(bf16)
                lo = (log_fail - hi.astype(f32)).astype(bf16)
                off = run_ref[rr, :]
                shifted = [None] * nsub
                for b in reversed(range(nsub)):
                    sl = slice(b * LANES, (b + 1) * LANES)
                    hl_ref[rr, 2 * b * LANES:(2 * b + 1) * LANES] = hi[:, sl]
                    hl_ref[rr, (2 * b + 1) * LANES:(2 * b + 2) * LANES] = lo[:, sl]
                    shifted[b] = log_hit[:, sl] + off
                    off = off + jnp.sum(log_fail[:, sl], axis=1, keepdims=True)
                run_ref[rr, :] = off
                z_ref[rr, :] = jnp.concatenate(shifted, axis=1)
            for b in range(nsub):
                sl = slice(b * LANES, (b + 1) * LANES)
                inside = _dot(hl_ref[:, 2 * b * LANES:(2 * b + 2) * LANES], after_op)
                w = jnp.exp(z_ref[:, sl] + inside)
                if masked:
                    w = jnp.where(b * LANES + lane_all < row_all, w, 0.0)
                p_ref[:, sl] = w.astype(bf16)
            acc_ref[...] += _dot(p_ref[...], v_ref[0, cols, :])
            return (jnp.max(run_ref[...]) >= -SKIP_NATS).astype(jnp.int32)

        alive = step(i, True)
        lax.while_loop(lambda c: jnp.logical_and(c[0] >= 0, c[1] > 0),
                       lambda c: (c[0] - 1, step(c[0], False)), (i - 1, alive))

        o = jnp.where(low, acc_ref[0:blk], acc_ref[blk:2 * blk])
        o_ref[0, rows, :] = (o * g_ref[0, rows, :].astype(f32)).astype(o_ref.dtype)
        return 0

    lax.fori_loop(0, seq // blk, q_block, 0)


def _attn_c(qc, kc, vc, gc, blk):
    bsz, seq, width = qc.shape
    pairs = width // LANES
    assert seq % blk == 0
    spec = pl.BlockSpec((1, seq, LANES), lambda b, h: (b, 0, h))
    return pl.pallas_call(
        functools.partial(_attn_c_kernel, blk=blk),
        grid=(bsz, pairs),
        in_specs=[spec, spec, spec, spec],
        out_specs=spec,
        out_shape=jax.ShapeDtypeStruct((bsz, seq, width), jnp.bfloat16),
        scratch_shapes=[pltpu.VMEM((2 * blk, LANES), jnp.float32), pltpu.VMEM((2 * blk, LANES), jnp.float32),
                        pltpu.VMEM((2 * blk, blk), jnp.float32), pltpu.VMEM((2 * blk, 2 * blk), jnp.bfloat16),
                        pltpu.VMEM((2 * blk, blk), jnp.bfloat16)],
        compiler_params=pltpu.CompilerParams(
            dimension_semantics=("arbitrary", "arbitrary"), vmem_limit_bytes=VMEM_LIMIT),
        name="attn_stick",
    )(qc, kc, vc, gc)


def _outproj_kernel(ya_ref, yb_ref, yc_ref, h_ref, w_ref, g_ref, b_ref, o_ref, *, alpha):
    wa, wb = ya_ref.shape[1], yb_ref.shape[1]
    y = (_dot(ya_ref[...], w_ref[0:wa, :]) + _dot(yb_ref[...], w_ref[wa:wa + wb, :])
         + _dot(yc_ref[...], w_ref[wa + wb:, :]))
    o_ref[...] = _layernorm_rows(alpha * h_ref[...] + y, g_ref[...], b_ref[...])


def _outproj(ya, yb, yc, h2d, w, g, b, alpha, tm):
    m, d = h2d.shape
    row = lambda a: pl.BlockSpec((tm, a.shape[1]), lambda i: (i, 0))
    full = lambda a: pl.BlockSpec(a.shape, lambda i: (0, 0))
    return pl.pallas_call(
        functools.partial(_outproj_kernel, alpha=alpha),
        grid=(m // tm,),
        in_specs=[row(ya), row(yb), row(yc), row(h2d), full(w), full(g), full(b)],
        out_specs=pl.BlockSpec((tm, d), lambda i: (i, 0)),
        out_shape=jax.ShapeDtypeStruct((m, d), jnp.float32),
        compiler_params=pltpu.CompilerParams(vmem_limit_bytes=VMEM_LIMIT),
        name="out_proj_ln",
    )(ya, yb, yc, h2d, w, g, b)


def _rope_tables(seq):
    half = ROT_DIM // 2
    inv = ROPE_THETA ** (-jnp.arange(0, ROT_DIM, 2, dtype=jnp.float32) / ROT_DIM)
    ang = jnp.arange(seq, dtype=jnp.float32)[:, None] * inv[None, :]
    cos, sin = jnp.cos(ang), jnp.sin(ang)
    pad = HEAD_DIM - ROT_DIM
    one_head = lambda a, b, fill: jnp.concatenate(
        [a, b, jnp.full((seq, pad), fill, jnp.float32)], axis=1)
    zeros = jnp.zeros((seq, half), jnp.float32)
    reps = LANES // HEAD_DIM
    cos_t = jnp.tile(one_head(cos, cos, 1.0), (1, reps))
    sin_hi = jnp.tile(one_head(zeros, sin, 0.0), (1, reps))
    sin_lo = jnp.tile(one_head(-sin, zeros, 0.0), (1, reps))
    return cos_t, sin_hi, sin_lo


def kernel(x, ln_in_g, ln_in_b, w_in, b_forget, lambda_q1, lambda_k1, lambda_q2, lambda_k2,
           subln_g, w_out, ln_g, ln_b):
    bsz, seq, d = x.shape
    depth = w_in.shape[0]
    hb = b_forget.shape[1]
    d_mix = w_out.shape[1]
    wa, wb, wc = d_mix // 2, d_mix // 4, d_mix // 4
    assert w_in.shape[2] == 4 * wa + 4 * wb + hb + 4 * wc and wb == hb * HEAD_DIM
    alpha = (2 * depth) ** 0.25
    tm = min(512, seq)
    blk = min(512, seq)
    kw = min(512, seq)
    bf16 = jnp.bfloat16

    cos, shi, slo = _rope_tables(seq)
    tri = (jnp.arange(tm)[:, None] >= jnp.arange(tm)[None, :]).astype(bf16)

    h = _layernorm(x.reshape(bsz * seq, d), ln_in_g, ln_in_b, tm).reshape(bsz, seq, d)
    for layer in range(depth):
        w = w_in[layer]
        o_b = 4 * wa
        o_f = o_b + 3 * wb
        o_g = o_f + hb
        o_c = o_g + wb
        spread = lambda a: jnp.pad(a.reshape(a.shape[:-1] + (hb // 2, 2)),
                                   [(0, 0)] * (a.ndim - 1)
                                   + [(0, LANES // SUBLANES - hb // 2), (0, SUBLANES - 2)]
                                   ).reshape(a.shape[:-1] + (LANES,))
        w_a, w_b, w_f, w_c, w_o = _cast_bf16(
            [w[:, :o_b], jnp.concatenate([w[:, o_b:o_f], w[:, o_g:o_c]], axis=1),
             spread(w[:, o_f:o_g]), w[:, o_c:], w_out[layer]], min(256, d))
        b_f = spread(b_forget[layer]).reshape(1, LANES)

        (qa, ka, va, ga, qb, kb, vb, gb, c_col, c_row, qc, kc, vc, gc) = _inproj(
            h, w_a, w_b, w_f, w_c, b_f, cos, shi, slo, tri, tm)
        c_end = c_row[:, :, :, kw - 1::kw]

        lam_init = 0.8 - 0.6 * math.exp(-0.3 * layer)
        lam_vecs = jnp.stack([lambda_q1[layer], lambda_k1[layer], lambda_q2[layer], lambda_k2[layer]])
        ya = _attn_a(lam_vecs, subln_g[layer].reshape(1, LANES), qa, ka, va, ga, lam_init, blk, kw)
        yb = _attn_b(qb, kb, vb, gb, c_col, c_row, c_end, blk, kw)
        yc = _attn_c(qc, kc, vc, gc, blk)

        m = bsz * seq
        h = _outproj(ya.reshape(m, wa), yb.reshape(m, wb), yc.reshape(m, wc), h.reshape(m, d),
                     w_o, ln_g[layer].reshape(1, d), ln_b[layer].reshape(1, d),
                     alpha, tm).reshape(bsz, seq, d)
    return h
```

```python
import functools
import math

import jax
import jax.numpy as jnp
from jax import lax
from jax.experimental import pallas as pl
from jax.experimental.pallas import tpu as pltpu

HEAD_DIM = 64
LANES = 128
SUBLANES = 8
ROT_DIM = HEAD_DIM // 4
ROPE_THETA = 500000.0
LN_EPS = 1e-5
SUBLN_EPS = 1e-5
LOG2E = 1.4426950408889634
NEG_BIG = -1e30
SKIP_NATS = 100.0
VMEM_LIMIT = 56 * 1024 * 1024
SOFTMAX_ROWS = 32

_NT = (((1,), (1,)), ((), ()))


def _dot(a, b):
    return jnp.dot(a, b, preferred_element_type=jnp.float32)


def _dot_nt(a, b):
    return lax.dot_general(a, b, _NT, preferred_element_type=jnp.float32)


def _layernorm_rows(t, g, b):
    mu = jnp.mean(t, axis=-1, keepdims=True)
    d = t - mu
    var = jnp.mean(d * d, axis=-1, keepdims=True)
    return d * lax.rsqrt(var + LN_EPS) * g + b


def _ln_kernel(x_ref, g_ref, b_ref, o_ref):
    o_ref[...] = _layernorm_rows(x_ref[...], g_ref[...], b_ref[...])


def _layernorm(x2d, g, b, tm):
    m, d = x2d.shape
    return pl.pallas_call(
        _ln_kernel,
        grid=(m // tm,),
        in_specs=[pl.BlockSpec((tm, d), lambda i: (i, 0)),
                  pl.BlockSpec((1, d), lambda i: (0, 0)),
                  pl.BlockSpec((1, d), lambda i: (0, 0))],
        out_specs=pl.BlockSpec((tm, d), lambda i: (i, 0)),
        out_shape=jax.ShapeDtypeStruct((m, d), jnp.float32),
        compiler_params=pltpu.CompilerParams(vmem_limit_bytes=VMEM_LIMIT),
        name="ln_in",
    )(x2d, g.reshape(1, d), b.reshape(1, d))


def _cast_kernel(*refs):
    n = len(refs) // 2
    for src, dst in zip(refs[:n], refs[n:]):
        dst[...] = src[...].astype(dst.dtype)


def _cast_bf16(arrays, tr):
    rows = arrays[0].shape[0]
    spec = lambda a: pl.BlockSpec((tr, a.shape[1]), lambda i: (i, 0))
    return pl.pallas_call(
        _cast_kernel,
        grid=(rows // tr,),
        in_specs=[spec(a) for a in arrays],
        out_specs=[spec(a) for a in arrays],
        out_shape=[jax.ShapeDtypeStruct(a.shape, jnp.bfloat16) for a in arrays],
        compiler_params=pltpu.CompilerParams(vmem_limit_bytes=VMEM_LIMIT),
        name="cast_weights",
    )(*arrays)


def _rope(t, cos, sin_hi, sin_lo):
    out = []
    for c in range(t.shape[1] // LANES):
        x = t[:, c * LANES:(c + 1) * LANES]
        out.append(x * cos
                   + pltpu.roll(x, ROT_DIM // 2, axis=1) * sin_hi
                   + pltpu.roll(x, LANES - ROT_DIM // 2, axis=1) * sin_lo)
    return jnp.concatenate(out, axis=1)


def _silu(t):
    return t * jax.nn.sigmoid(t)


def _softplus(t):
    return jnp.maximum(t, 0.0) + jnp.log1p(jnp.exp(-jnp.abs(t)))


def _inproj_kernel(h_ref, wa_ref, wb_ref, wf_ref, wc_ref, bf_ref, cos_ref, shi_ref, slo_ref, tri_ref,
                   qa_ref, ka_ref, va_ref, ga_ref, qb_ref, kb_ref, vb_ref, gb_ref, cb_ref, cr_ref,
                   qc_ref, kc_ref, vc_ref, gc_ref, carry_ref, *, wa, wb, wc):
    scale = HEAD_DIM ** -0.5
    hb = h_ref[0].astype(jnp.bfloat16)
    cos, shi, slo = cos_ref[...], shi_ref[...], slo_ref[...]
    bf16 = jnp.bfloat16

    qa_ref[0] = (_rope(_dot(hb, wa_ref[:, 0:wa]), cos, shi, slo) * (scale * LOG2E)).astype(bf16)
    ka_ref[0] = _rope(_dot(hb, wa_ref[:, wa:2 * wa]), cos, shi, slo).astype(bf16)
    va_ref[0] = _dot(hb, wa_ref[:, 2 * wa:3 * wa]).astype(bf16)
    ga_ref[0] = _silu(_dot(hb, wa_ref[:, 3 * wa:4 * wa])).astype(bf16)

    qb_ref[0] = (_dot(hb, wb_ref[:, 0:wb]) * (scale * LOG2E)).astype(bf16)
    kb_ref[0] = _dot(hb, wb_ref[:, wb:2 * wb]).astype(bf16)
    vb_ref[0] = _dot(hb, wb_ref[:, 2 * wb:3 * wb]).astype(bf16)
    gb_ref[0] = _silu(_dot(hb, wb_ref[:, 3 * wb:4 * wb])).astype(bf16)

    @pl.when(pl.program_id(1) == 0)
    def _():
        carry_ref[...] = jnp.zeros_like(carry_ref)

    logf = -_softplus(-(_dot(hb, wf_ref[...]) + bf_ref[...]))
    p1 = logf.astype(bf16)
    r1 = logf - p1.astype(jnp.float32)
    p2 = r1.astype(bf16)
    p3 = (r1 - p2.astype(jnp.float32)).astype(bf16)
    cs = _dot(tri_ref[...], jnp.concatenate([p1, p2, p3], axis=1))
    c = cs[:, 0:LANES] + cs[:, LANES:2 * LANES] + cs[:, 2 * LANES:3 * LANES] + carry_ref[...]
    carry_ref[...] = c[c.shape[0] - 1:, :]
    c2 = c * LOG2E
    cb_ref[0] = c2
    c2t = c2.T
    for p in range(cr_ref.shape[1]):
        cr_ref[0, p] = c2t[SUBLANES * p:SUBLANES * (p + 1), :]

    qc_ref[0] = (_dot(hb, wc_ref[:, 0:wc]) * (scale * LOG2E)).astype(bf16)
    kc_ref[0] = _dot(hb, wc_ref[:, wc:2 * wc]).astype(bf16)
    vc_ref[0] = _dot(hb, wc_ref[:, 2 * wc:3 * wc]).astype(bf16)
    gc_ref[0] = _silu(_dot(hb, wc_ref[:, 3 * wc:4 * wc])).astype(bf16)


def _inproj(h, w_a, w_b, w_f, w_c, b_f, cos, shi, slo, tri, tm):
    bsz, seq, d = h.shape
    wa, wb, wc = w_a.shape[1] // 4, w_b.shape[1] // 4, w_c.shape[1] // 4
    row = lambda n: pl.BlockSpec((1, tm, n), lambda b, t: (b, t, 0))
    full = lambda a: pl.BlockSpec(a.shape, lambda b, t: (0,) * a.ndim)
    tab = pl.BlockSpec((tm, LANES), lambda b, t: (t, 0))
    bf16 = jnp.bfloat16
    shp = lambda n, dt=bf16: jax.ShapeDtypeStruct((bsz, seq, n), dt)
    pairs = wb // LANES
    c_row_spec = pl.BlockSpec((1, pairs, SUBLANES, tm), lambda b, t: (b, 0, 0, t))
    c_row_shape = jax.ShapeDtypeStruct((bsz, pairs, SUBLANES, seq), jnp.float32)
    return pl.pallas_call(
        functools.partial(_inproj_kernel, wa=wa, wb=wb, wc=wc),
        grid=(bsz, seq // tm),
        in_specs=[row(d), full(w_a), full(w_b), full(w_f), full(w_c), full(b_f), tab, tab, tab, full(tri)],
        out_specs=[row(wa)] * 4 + [row(wb)] * 4 + [row(LANES), c_row_spec] + [row(wc)] * 4,
        out_shape=([shp(wa)] * 4 + [shp(wb)] * 4 + [shp(LANES, jnp.float32), c_row_shape]
                   + [shp(wc)] * 4),
        scratch_shapes=[pltpu.VMEM((1, LANES), jnp.float32)],
        compiler_params=pltpu.CompilerParams(
            dimension_semantics=("arbitrary", "arbitrary"), vmem_limit_bytes=VMEM_LIMIT),
        name="in_proj",
    )(h, w_a, w_b, w_f, w_c, b_f, cos, shi, slo, tri)


def _head_masks(rows):
    lane = lax.broadcasted_iota(jnp.int32, (rows, LANES), 1)
    return lane < HEAD_DIM


def _stack_heads(q, low):
    zero = jnp.zeros_like(q)
    return jnp.concatenate([jnp.where(low, q, zero), jnp.where(low, zero, q)], axis=0)


def _lane_tile(t, width):
    return jnp.concatenate([t] * (width // LANES), axis=1)


def _lane_fold(t, op=jnp.add):
    out = t[:, 0:LANES]
    for c in range(1, t.shape[1] // LANES):
        out = op(out, t[:, c * LANES:(c + 1) * LANES])
    return out


def _flash_scratch(blk, kw):
    return ([pltpu.VMEM((2 * blk, LANES), jnp.float32)] * 4
            + [pltpu.VMEM((2 * blk, kw), jnp.float32)] * 2
            + [pltpu.VMEM((2 * blk, kw), jnp.bfloat16)] * 2)


def _causal_flash(i, j_start, q_st, k_ref, v_ref, scratch, *, blk, kw, bias=None):
    m_ref, l_ref, acc_ref, alpha_ref, s0_ref, s1_ref, p0_ref, p1_ref = scratch
    s_refs, p_refs = (s0_ref, s1_ref), (p0_ref, p1_ref)
    row_pos = lax.broadcasted_iota(jnp.int32, (SOFTMAX_ROWS, kw), 0)
    col_pos = lax.broadcasted_iota(jnp.int32, (SOFTMAX_ROWS, kw), 1)
    n_diag = blk // kw
    n_full = i * n_diag
    m_ref[...] = jnp.full_like(m_ref, NEG_BIG)
    l_ref[...] = jnp.zeros_like(l_ref)
    acc_ref[...] = jnp.zeros_like(acc_ref)

    def cols(j):
        return pl.ds(pl.multiple_of(j * kw, kw), kw)

    def scores(j):
        return _dot_nt(q_st, k_ref[0, cols(j), :])

    def softmax(s_ref, p_ref, j, diag_tile):
        for r0 in range(0, 2 * blk, SOFTMAX_ROWS):
            rr = pl.ds(r0, SOFTMAX_ROWS)
            q0 = r0 % blk
            s = s_ref[rr, :]
            if bias is not None:
                s = s + bias(r0, j * kw)
            if diag_tile is not None and q0 < (diag_tile + 1) * kw - 1:
                s = jnp.where(diag_tile * kw + col_pos <= q0 + row_pos, s, NEG_BIG)
            m_old = m_ref[rr, :]
            m_new = jnp.maximum(m_old, jnp.max(_lane_fold(s, jnp.maximum), axis=1, keepdims=True))
            alpha = jnp.exp2(m_old - m_new)
            alpha_ref[rr, :] = alpha
            p = jnp.exp2(s - _lane_tile(m_new, kw))
            l_ref[rr, :] = alpha * l_ref[rr, :] + _lane_fold(p)
            m_ref[rr, :] = m_new
            p_ref[rr, :] = p.astype(jnp.bfloat16)

    def stage(j, par, diag_tile=None):
        pv = _dot(p_refs[1 - par][...], v_ref[0, cols(jnp.maximum(j - 1, 0)), :])
        if diag_tile is None or diag_tile + 1 < n_diag:
            s_refs[1 - par][...] = scores(j + 1)
        softmax(s_refs[par], p_refs[par], j, diag_tile)
        acc_ref[...] = alpha_ref[...] * (acc_ref[...] + pv)
        if diag_tile is not None and diag_tile + 1 == n_diag:
            acc_ref[...] += _dot(p_refs[par][...], v_ref[0, cols(j), :])

    def by_parity(j, fn):
        for par in range(2):
            pl.when(j % 2 == par)(functools.partial(fn, par))

    def prologue(par):
        s_refs[par][...] = scores(j_start)
        p_refs[1 - par][...] = jnp.zeros_like(p_refs[1 - par])

    if isinstance(j_start, int) and j_start == 0 and n_diag % 2 == 0:
        prologue(0)

        def pair(jj, carry):
            stage(2 * jj, 0)
            stage(2 * jj + 1, 1)
            return carry

        lax.fori_loop(0, n_full // 2, pair, 0)
        for t in range(n_diag):
            stage(n_full + t, t % 2, t)
    else:
        by_parity(j_start, prologue)

        def body(j, carry):
            by_parity(j, lambda par: stage(j, par))
            return carry

        lax.fori_loop(j_start, n_full, body, 0)
        for t in range(n_diag):
            by_parity(n_full + t, lambda par, t=t: stage(n_full + t, par, t))


def _attn_a_kernel(lam_ref, sg_ref, q_ref, k_ref, v_ref, g_ref, o_ref, *scratch, blk, kw, lam_init):
    seq = q_ref.shape[1]
    low = _head_masks(blk)
    l_ref, acc_ref = scratch[1], scratch[2]
    lv = lam_ref[...]
    lam = (jnp.exp(jnp.sum(lv[0:1] * lv[1:2], axis=1, keepdims=True))
           - jnp.exp(jnp.sum(lv[2:3] * lv[3:4], axis=1, keepdims=True)) + lam_init)

    def q_block(i, _):
        rows = pl.ds(pl.multiple_of(i * blk, blk), blk)
        q_st = _stack_heads(q_ref[0, rows, :], low)
        _causal_flash(i, 0, q_st, k_ref, v_ref, scratch, blk=blk, kw=kw)
        o_st = acc_ref[...] / jnp.sum(l_ref[...], axis=1, keepdims=True)
        o = o_st[0:blk] - lam * o_st[blk:2 * blk]
        o = o * lax.rsqrt(jnp.mean(o * o, axis=1, keepdims=True) + SUBLN_EPS)
        o = o * (sg_ref[...] * (1.0 - lam_init))
        o_ref[0, rows, :] = (o * g_ref[0, rows, :].astype(jnp.float32)).astype(o_ref.dtype)
        return 0

    lax.fori_loop(0, seq // blk, q_block, 0)


def _attn_a(lam_vecs, subln_g, qa, ka, va, ga, lam_init, blk, kw):
    bsz, seq, width = qa.shape
    heads = width // LANES
    assert seq % blk == 0 and blk % kw == 0
    spec = pl.BlockSpec((1, seq, LANES), lambda b, h: (b, 0, h))
    small = lambda a: pl.BlockSpec(a.shape, lambda b, h: (0, 0))
    return pl.pallas_call(
        functools.partial(_attn_a_kernel, blk=blk, kw=kw, lam_init=lam_init),
        grid=(bsz, heads),
        in_specs=[small(lam_vecs), small(subln_g), spec, spec, spec, spec],
        out_specs=spec,
        out_shape=jax.ShapeDtypeStruct((bsz, seq, width), jnp.bfloat16),
        scratch_shapes=_flash_scratch(blk, kw),
        compiler_params=pltpu.CompilerParams(
            dimension_semantics=("arbitrary", "arbitrary"), vmem_limit_bytes=VMEM_LIMIT),
        name="attn_diff",
    )(lam_vecs, subln_g, qa, ka, va, ga)


def _attn_b_kernel(q_ref, k_ref, v_ref, g_ref, cc_ref, cr_ref, ce_ref, o_ref, cq_ref, *scratch, blk, kw):
    seq = q_ref.shape[1]
    pair = pl.program_id(1)
    low = _head_masks(blk)
    lane = lax.broadcasted_iota(jnp.int32, (1, LANES), 1)
    l_ref, acc_ref = scratch[1], scratch[2]
    f32 = jnp.float32

    def kn_block(j, kn2):
        kk = k_ref[0, pl.ds(pl.multiple_of(j * blk, blk), blk), :].astype(f32)
        sq = kk * kk
        zero = jnp.zeros_like(sq)
        n_lo = jnp.max(jnp.sum(jnp.where(low, sq, zero), axis=1, keepdims=True), axis=0, keepdims=True)
        n_hi = jnp.max(jnp.sum(jnp.where(low, zero, sq), axis=1, keepdims=True), axis=0, keepdims=True)
        return jnp.maximum(kn2[0], n_lo), jnp.maximum(kn2[1], n_hi)

    kn2 = lax.fori_loop(0, seq // blk, kn_block, (jnp.zeros((1, 1), f32), jnp.zeros((1, 1), f32)))
    kn = (jnp.sqrt(kn2[0]), jnp.sqrt(kn2[1]))

    def q_block(i, _):
        rows = pl.ds(pl.multiple_of(i * blk, blk), blk)
        q_st = _stack_heads(q_ref[0, rows, :], low)
        qf = q_st.astype(f32)
        kf = k_ref[0, rows, :].astype(f32)
        qn = jnp.sqrt(jnp.sum(qf * qf, axis=1, keepdims=True))
        diag = jnp.sum(qf * jnp.concatenate([kf, kf], axis=0), axis=1, keepdims=True)
        cc = cc_ref[0, rows, :]
        cq = jnp.concatenate(
            [jnp.sum(jnp.where(lane == SUBLANES * pair + h, cc, 0.0), axis=1, keepdims=True)
             for h in range(2)], axis=0)
        cq_ref[...] = jnp.broadcast_to(cq, cq_ref.shape)

        n_full = (i * blk) // kw
        j_start = n_full
        for h in range(2):
            hr = slice(h * blk, (h + 1) * blk)
            reach = jnp.max(qn[hr] * kn[h] + cq[hr] - diag[hr], axis=0, keepdims=True)
            skipped = (ce_ref[0, 0, h:h + 1, :] > reach + SKIP_NATS * LOG2E).astype(jnp.int32)
            j_start = jnp.minimum(j_start, jnp.sum(skipped))

        def bias(r0, start):
            h = r0 // blk
            ck = cr_ref[0, 0, h:h + 1, pl.ds(pl.multiple_of(start, LANES), kw)]
            return _lane_tile(cq_ref[pl.ds(r0, SOFTMAX_ROWS), :], kw) - ck

        _causal_flash(i, j_start, q_st, k_ref, v_ref, scratch, blk=blk, kw=kw, bias=bias)
        o_st = acc_ref[...] / jnp.sum(l_ref[...], axis=1, keepdims=True)
        o = jnp.where(low, o_st[0:blk], o_st[blk:2 * blk])
        o_ref[0, rows, :] = (o * g_ref[0, rows, :].astype(f32)).astype(o_ref.dtype)
        return 0

    lax.fori_loop(0, seq // blk, q_block, 0)


def _attn_b(qb, kb, vb, gb, c_col, c_row, c_end, blk, kw):
    bsz, seq, width = qb.shape
    pairs = width // LANES
    assert seq % blk == 0 and blk % kw == 0
    spec = pl.BlockSpec((1, seq, LANES), lambda b, h: (b, 0, h))
    per_pair = lambda a: pl.BlockSpec((1, 1) + a.shape[2:], lambda b, h: (b, h, 0, 0))
    return pl.pallas_call(
        functools.partial(_attn_b_kernel, blk=blk, kw=kw),
        grid=(bsz, pairs),
        in_specs=[spec, spec, spec, spec,
                  pl.BlockSpec((1, seq, LANES), lambda b, h: (b, 0, 0)),
                  per_pair(c_row), per_pair(c_end)],
        out_specs=spec,
        out_shape=jax.ShapeDtypeStruct((bsz, seq, width), jnp.bfloat16),
        scratch_shapes=[pltpu.VMEM((2 * blk, LANES), jnp.float32)] + _flash_scratch(blk, kw),
        compiler_params=pltpu.CompilerParams(
            dimension_semantics=("arbitrary", "arbitrary"), vmem_limit_bytes=VMEM_LIMIT),
        name="attn_forget",
    )(qb, kb, vb, gb, c_col, c_row, c_end)


def _attn_c_kernel(q_ref, k_ref, v_ref, g_ref, o_ref, run_ref, acc_ref, z_ref, hl_ref, p_ref, *, blk):
    seq = q_ref.shape[1]
    nsub = blk // LANES
    low = _head_masks(blk)
    f32, bf16 = jnp.float32, jnp.bfloat16
    jj = lax.broadcasted_iota(jnp.int32, (2 * LANES, LANES), 0) % LANES
    ss = lax.broadcasted_iota(jnp.int32, (2 * LANES, LANES), 1)
    after_op = (jj > ss).astype(bf16)
    row_pos = lax.broadcasted_iota(jnp.int32, (SOFTMAX_ROWS, blk), 0)
    col_pos = lax.broadcasted_iota(jnp.int32, (SOFTMAX_ROWS, blk), 1)
    row_all = lax.broadcasted_iota(jnp.int32, (2 * blk, LANES), 0) % blk
    lane_all = lax.broadcasted_iota(jnp.int32, (2 * blk, LANES), 1)

    def q_block(i, _):
        rows = pl.ds(pl.multiple_of(i * blk, blk), blk)
        q_st = _stack_heads(q_ref[0, rows, :], low)
        run_ref[...] = jnp.zeros_like(run_ref)
        acc_ref[...] = jnp.zeros_like(acc_ref)

        def step(j, masked):
            cols = pl.ds(pl.multiple_of(j * blk, blk), blk)
            z_ref[...] = _dot_nt(q_st, k_ref[0, cols, :])
            for r0 in range(0, 2 * blk, SOFTMAX_ROWS):
                rr = pl.ds(r0, SOFTMAX_ROWS)
                z = z_ref[rr, :]
                miss = jnp.maximum(z, 0.0) + jnp.log2(1.0 + jnp.exp2(-jnp.abs(z)))
                log_hit = z - miss
                if masked:
                    miss = jnp.where(col_pos < (r0 % blk) + row_pos, miss, 0.0)
                hi = miss.astype(bf16)
                lo = (miss - hi.astype(f32)).astype(bf16)
                off = run_ref[rr, :]
                shifted = [None] * nsub
                for b in reversed(range(nsub)):
                    sl = slice(b * LANES, (b + 1) * LANES)
                    hl_ref[rr, 2 * b * LANES:(2 * b + 1) * LANES] = hi[:, sl]
                    hl_ref[rr, (2 * b + 1) * LANES:(2 * b + 2) * LANES] = lo[:, sl]
                    shifted[b] = log_hit[:, sl] - off
                    off = off + jnp.sum(miss[:, sl], axis=1, keepdims=True)
                run_ref[rr, :] = off
                z_ref[rr, :] = jnp.concatenate(shifted, axis=1)
            for b in range(nsub):
                sl = slice(b * LANES, (b + 1) * LANES)
                inside = _dot(hl_ref[:, 2 * b * LANES:(2 * b + 2) * LANES], after_op)
                w = jnp.exp2(z_ref[:, sl] - inside)
                if masked:
                    w = jnp.where(b * LANES + lane_all < row_all, w, 0.0)
                p_ref[:, sl] = w.astype(bf16)
            acc_ref[...] += _dot(p_ref[...], v_ref[0, cols, :])
            return (jnp.min(run_ref[...]) <= SKIP_NATS * LOG2E).astype(jnp.int32)

        alive = step(i, True)
        lax.while_loop(lambda c: jnp.logical_and(c[0] >= 0, c[1] > 0),
                       lambda c: (c[0] - 1, step(c[0], False)), (i - 1, alive))

        o = jnp.where(low, acc_ref[0:blk], acc_ref[blk:2 * blk])
        o_ref[0, rows, :] = (o * g_ref[0, rows, :].astype(f32)).astype(o_ref.dtype)
        return 0

    lax.fori_loop(0, seq // blk, q_block, 0)


def _attn_c(qc, kc, vc, gc, blk):
    bsz, seq, width = qc.shape
    pairs = width // LANES
    assert seq % blk == 0
    spec = pl.BlockSpec((1, seq, LANES), lambda b, h: (b, 0, h))
    return pl.pallas_call(
        functools.partial(_attn_c_kernel, blk=blk),
        grid=(bsz, pairs),
        in_specs=[spec, spec, spec, spec],
        out_specs=spec,
        out_shape=jax.ShapeDtypeStruct((bsz, seq, width), jnp.bfloat16),
        scratch_shapes=[pltpu.VMEM((2 * blk, LANES), jnp.float32), pltpu.VMEM((2 * blk, LANES), jnp.float32),
                        pltpu.VMEM((2 * blk, blk), jnp.float32), pltpu.VMEM((2 * blk, 2 * blk), jnp.bfloat16),
                        pltpu.VMEM((2 * blk, blk), jnp.bfloat16)],
        compiler_params=pltpu.CompilerParams(
            dimension_semantics=("arbitrary", "arbitrary"), vmem_limit_bytes=VMEM_LIMIT),
        name="attn_stick",
    )(qc, kc, vc, gc)


def _outproj_kernel(ya_ref, yb_ref, yc_ref, h_ref, w_ref, g_ref, b_ref, o_ref, *, alpha):
    wa, wb = ya_ref.shape[1], yb_ref.shape[1]
    y = (_dot(ya_ref[...], w_ref[0:wa, :]) + _dot(yb_ref[...], w_ref[wa:wa + wb, :])
         + _dot(yc_ref[...], w_ref[wa + wb:, :]))
    o_ref[...] = _layernorm_rows(alpha * h_ref[...] + y, g_ref[...], b_ref[...])


def _outproj(ya, yb, yc, h2d, w, g, b, alpha, tm):
    m, d = h2d.shape
    row = lambda a: pl.BlockSpec((tm, a.shape[1]), lambda i: (i, 0))
    full = lambda a: pl.BlockSpec(a.shape, lambda i: (0, 0))
    return pl.pallas_call(
        functools.partial(_outproj_kernel, alpha=alpha),
        grid=(m // tm,),
        in_specs=[row(ya), row(yb), row(yc), row(h2d), full(w), full(g), full(b)],
        out_specs=pl.BlockSpec((tm, d), lambda i: (i, 0)),
        out_shape=jax.ShapeDtypeStruct((m, d), jnp.float32),
        compiler_params=pltpu.CompilerParams(vmem_limit_bytes=VMEM_LIMIT),
        name="out_proj_ln",
    )(ya, yb, yc, h2d, w, g, b)


def _rope_tables(seq):
    half = ROT_DIM // 2
    inv = ROPE_THETA ** (-jnp.arange(0, ROT_DIM, 2, dtype=jnp.float32) / ROT_DIM)
    ang = jnp.arange(seq, dtype=jnp.float32)[:, None] * inv[None, :]
    cos, sin = jnp.cos(ang), jnp.sin(ang)
    pad = HEAD_DIM - ROT_DIM
    one_head = lambda a, b, fill: jnp.concatenate(
        [a, b, jnp.full((seq, pad), fill, jnp.float32)], axis=1)
    zeros = jnp.zeros((seq, half), jnp.float32)
    reps = LANES // HEAD_DIM
    cos_t = jnp.tile(one_head(cos, cos, 1.0), (1, reps))
    sin_hi = jnp.tile(one_head(zeros, sin, 0.0), (1, reps))
    sin_lo = jnp.tile(one_head(-sin, zeros, 0.0), (1, reps))
    return cos_t, sin_hi, sin_lo


def kernel(x, ln_in_g, ln_in_b, w_in, b_forget, lambda_q1, lambda_k1, lambda_q2, lambda_k2,
           subln_g, w_out, ln_g, ln_b):
    bsz, seq, d = x.shape
    depth = w_in.shape[0]
    hb = b_forget.shape[1]
    d_mix = w_out.shape[1]
    wa, wb, wc = d_mix // 2, d_mix // 4, d_mix // 4
    assert w_in.shape[2] == 4 * wa + 4 * wb + hb + 4 * wc and wb == hb * HEAD_DIM
    alpha = (2 * depth) ** 0.25
    tm = min(512, seq)
    blk = min(512, seq)
    blk_a = min(1024, seq)
    kw = min(512, seq)
    bf16 = jnp.bfloat16

    cos, shi, slo = _rope_tables(seq)
    tri = (jnp.arange(tm)[:, None] >= jnp.arange(tm)[None, :]).astype(bf16)

    h = _layernorm(x.reshape(bsz * seq, d), ln_in_g, ln_in_b, tm).reshape(bsz, seq, d)
    for layer in range(depth):
        w = w_in[layer]
        o_b = 4 * wa
        o_f = o_b + 3 * wb
        o_g = o_f + hb
        o_c = o_g + wb
        spread = lambda a: jnp.pad(a.reshape(a.shape[:-1] + (hb // 2, 2)),
                                   [(0, 0)] * (a.ndim - 1)
                                   + [(0, LANES // SUBLANES - hb // 2), (0, SUBLANES - 2)]
                                   ).reshape(a.shape[:-1] + (LANES,))
        w_a, w_b, w_f, w_c, w_o = _cast_bf16(
            [w[:, :o_b], jnp.concatenate([w[:, o_b:o_f], w[:, o_g:o_c]], axis=1),
             spread(w[:, o_f:o_g]), w[:, o_c:], w_out[layer]], min(256, d))
        b_f = spread(b_forget[layer]).reshape(1, LANES)

        (qa, ka, va, ga, qb, kb, vb, gb, c_col, c_row, qc, kc, vc, gc) = _inproj(
            h, w_a, w_b, w_f, w_c, b_f, cos, shi, slo, tri, tm)
        c_end = c_row[:, :, :, kw - 1::kw]

        lam_init = 0.8 - 0.6 * math.exp(-0.3 * layer)
        lam_vecs = jnp.stack([lambda_q1[layer], lambda_k1[layer], lambda_q2[layer], lambda_k2[layer]])
        ya = _attn_a(lam_vecs, subln_g[layer].reshape(1, LANES), qa, ka, va, ga, lam_init, blk_a, kw)
        yb = _attn_b(qb, kb, vb, gb, c_col, c_row, c_end, blk, kw)
        yc = _attn_c(qc, kc, vc, gc, blk)

        m = bsz * seq
        h = _outproj(ya.reshape(m, wa), yb.reshape(m, wb), yc.reshape(m, wc), h.reshape(m, d),
                     w_o, ln_g[layer].reshape(1, d), ln_b[layer].reshape(1, d),
                     alpha, tm).reshape(bsz, seq, d)
    return h
```

```python
import functools
import math

import jax
import jax.numpy as jnp
from jax import lax
from jax.experimental import pallas as pl
from jax.experimental.pallas import tpu as pltpu

HEAD_DIM = 64
LANES = 128
SUBLANES = 8
ROT_DIM = HEAD_DIM // 4
ROPE_THETA = 500000.0
LN_EPS = 1e-5
SUBLN_EPS = 1e-5
LOG2E = 1.4426950408889634
NEG_BIG = -1e30
SKIP_NATS = 100.0
VMEM_LIMIT = 56 * 1024 * 1024
SOFTMAX_ROWS = 32

_NT = (((1,), (1,)), ((), ()))


def _dot(a, b):
    return jnp.dot(a, b, preferred_element_type=jnp.float32)


def _dot_nt(a, b):
    return lax.dot_general(a, b, _NT, preferred_element_type=jnp.float32)


def _layernorm_rows(t, g, b):
    mu = jnp.mean(t, axis=-1, keepdims=True)
    d = t - mu
    var = jnp.mean(d * d, axis=-1, keepdims=True)
    return d * lax.rsqrt(var + LN_EPS) * g + b


def _cast_kernel(*refs):
    n = len(refs) // 2
    for src, dst in zip(refs[:n], refs[n:]):
        dst[...] = src[...].astype(dst.dtype)


def _cast_bf16(arrays, tr):
    rows = arrays[0].shape[0]
    spec = lambda a: pl.BlockSpec((tr, a.shape[1]), lambda i: (i, 0))
    return pl.pallas_call(
        _cast_kernel,
        grid=(rows // tr,),
        in_specs=[spec(a) for a in arrays],
        out_specs=[spec(a) for a in arrays],
        out_shape=[jax.ShapeDtypeStruct(a.shape, jnp.bfloat16) for a in arrays],
        compiler_params=pltpu.CompilerParams(vmem_limit_bytes=VMEM_LIMIT),
        name="cast_weights",
    )(*arrays)


def _rope(t, cos, sin_hi, sin_lo):
    out = []
    for c in range(t.shape[1] // LANES):
        x = t[:, c * LANES:(c + 1) * LANES]
        out.append(x * cos
                   + pltpu.roll(x, ROT_DIM // 2, axis=1) * sin_hi
                   + pltpu.roll(x, LANES - ROT_DIM // 2, axis=1) * sin_lo)
    return jnp.concatenate(out, axis=1)


def _silu(t):
    return t * jax.nn.sigmoid(t)


def _softplus(t):
    return jnp.maximum(t, 0.0) + jnp.log1p(jnp.exp(-jnp.abs(t)))


def _inproj_kernel(*refs, wa, wb, wc, pre_ln):
    if pre_ln:
        g_ref, b_ref, *refs = refs
    (h_ref, wa_ref, wb_ref, wf_ref, wc_ref, bf_ref, cos_ref, shi_ref, slo_ref, tri_ref,
     qa_ref, ka_ref, va_ref, ga_ref, qb_ref, kb_ref, vb_ref, gb_ref, cb_ref, cr_ref,
     qc_ref, kc_ref, vc_ref, gc_ref, *rest) = refs
    carry_ref = rest[-1]
    scale = HEAD_DIM ** -0.5
    h = h_ref[0]
    if pre_ln:
        h = _layernorm_rows(h, g_ref[...], b_ref[...])
        rest[0][0] = h
    hb = h.astype(jnp.bfloat16)
    cos, shi, slo = cos_ref[...], shi_ref[...], slo_ref[...]
    bf16 = jnp.bfloat16

    qa_ref[0] = (_rope(_dot(hb, wa_ref[:, 0:wa]), cos, shi, slo) * (scale * LOG2E)).astype(bf16)
    ka_ref[0] = _rope(_dot(hb, wa_ref[:, wa:2 * wa]), cos, shi, slo).astype(bf16)
    va_ref[0] = _dot(hb, wa_ref[:, 2 * wa:3 * wa]).astype(bf16)
    ga_ref[0] = _silu(_dot(hb, wa_ref[:, 3 * wa:4 * wa])).astype(bf16)

    qb_ref[0] = (_dot(hb, wb_ref[:, 0:wb]) * (scale * LOG2E)).astype(bf16)
    kb_ref[0] = _dot(hb, wb_ref[:, wb:2 * wb]).astype(bf16)
    vb_ref[0] = _dot(hb, wb_ref[:, 2 * wb:3 * wb]).astype(bf16)
    gb_ref[0] = _silu(_dot(hb, wb_ref[:, 3 * wb:4 * wb])).astype(bf16)

    @pl.when(pl.program_id(1) == 0)
    def _():
        carry_ref[...] = jnp.zeros_like(carry_ref)

    logf = -_softplus(-(_dot(hb, wf_ref[...]) + bf_ref[...]))
    p1 = logf.astype(bf16)
    r1 = logf - p1.astype(jnp.float32)
    p2 = r1.astype(bf16)
    p3 = (r1 - p2.astype(jnp.float32)).astype(bf16)
    cs = _dot(tri_ref[...], jnp.concatenate([p1, p2, p3], axis=1))
    c = cs[:, 0:LANES] + cs[:, LANES:2 * LANES] + cs[:, 2 * LANES:3 * LANES] + carry_ref[...]
    carry_ref[...] = c[c.shape[0] - 1:, :]
    c2 = c * LOG2E
    cb_ref[0] = c2
    c2t = c2.T
    for p in range(cr_ref.shape[1]):
        cr_ref[0, p] = c2t[SUBLANES * p:SUBLANES * (p + 1), :]

    qc_ref[0] = (_dot(hb, wc_ref[:, 0:wc]) * (scale * LOG2E)).astype(bf16)
    kc_ref[0] = _dot(hb, wc_ref[:, wc:2 * wc]).astype(bf16)
    vc_ref[0] = _dot(hb, wc_ref[:, 2 * wc:3 * wc]).astype(bf16)
    gc_ref[0] = _silu(_dot(hb, wc_ref[:, 3 * wc:4 * wc])).astype(bf16)


def _inproj(h, w_a, w_b, w_f, w_c, b_f, cos, shi, slo, tri, tm, ln=None):
    bsz, seq, d = h.shape
    wa, wb, wc = w_a.shape[1] // 4, w_b.shape[1] // 4, w_c.shape[1] // 4
    row = lambda n: pl.BlockSpec((1, tm, n), lambda b, t: (b, t, 0))
    full = lambda a: pl.BlockSpec(a.shape, lambda b, t: (0,) * a.ndim)
    tab = pl.BlockSpec((tm, LANES), lambda b, t: (t, 0))
    bf16 = jnp.bfloat16
    shp = lambda n, dt=bf16: jax.ShapeDtypeStruct((bsz, seq, n), dt)
    pairs = wb // LANES
    c_row_spec = pl.BlockSpec((1, pairs, SUBLANES, tm), lambda b, t: (b, 0, 0, t))
    c_row_shape = jax.ShapeDtypeStruct((bsz, pairs, SUBLANES, seq), jnp.float32)
    ln_args = [] if ln is None else [a.reshape(1, d) for a in ln]
    return pl.pallas_call(
        functools.partial(_inproj_kernel, wa=wa, wb=wb, wc=wc, pre_ln=ln is not None),
        grid=(bsz, seq // tm),
        in_specs=([full(a) for a in ln_args]
                  + [row(d), full(w_a), full(w_b), full(w_f), full(w_c), full(b_f), tab, tab, tab, full(tri)]),
        out_specs=([row(wa)] * 4 + [row(wb)] * 4 + [row(LANES), c_row_spec] + [row(wc)] * 4
                   + ([] if ln is None else [row(d)])),
        out_shape=([shp(wa)] * 4 + [shp(wb)] * 4 + [shp(LANES, jnp.float32), c_row_shape]
                   + [shp(wc)] * 4 + ([] if ln is None else [shp(d, jnp.float32)])),
        scratch_shapes=[pltpu.VMEM((1, LANES), jnp.float32)],
        compiler_params=pltpu.CompilerParams(
            dimension_semantics=("arbitrary", "arbitrary"), vmem_limit_bytes=VMEM_LIMIT),
        name="in_proj",
    )(*ln_args, h, w_a, w_b, w_f, w_c, b_f, cos, shi, slo, tri)


def _head_masks(rows):
    lane = lax.broadcasted_iota(jnp.int32, (rows, LANES), 1)
    return lane < HEAD_DIM


def _stack_heads(q, low):
    zero = jnp.zeros_like(q)
    return jnp.concatenate([jnp.where(low, q, zero), jnp.where(low, zero, q)], axis=0)


def _lane_tile(t, width):
    return jnp.concatenate([t] * (width // LANES), axis=1)


def _lane_fold(t, op=jnp.add):
    out = t[:, 0:LANES]
    for c in range(1, t.shape[1] // LANES):
        out = op(out, t[:, c * LANES:(c + 1) * LANES])
    return out


def _flash_scratch(blk, kw):
    return ([pltpu.VMEM((2 * blk, LANES), jnp.float32)] * 4
            + [pltpu.VMEM((2 * blk, kw), jnp.float32)] * 2
            + [pltpu.VMEM((2 * blk, kw), jnp.bfloat16)] * 2)


def _causal_flash(i, j_start, q_st, k_ref, v_ref, scratch, *, blk, kw, bias=None):
    m_ref, l_ref, acc_ref, alpha_ref, s0_ref, s1_ref, p0_ref, p1_ref = scratch
    s_refs, p_refs = (s0_ref, s1_ref), (p0_ref, p1_ref)
    row_pos = lax.broadcasted_iota(jnp.int32, (SOFTMAX_ROWS, kw), 0)
    col_pos = lax.broadcasted_iota(jnp.int32, (SOFTMAX_ROWS, kw), 1)
    n_diag = blk // kw
    n_full = i * n_diag
    trim = n_diag == 2
    live = blk - kw

    def live_rows(h):
        return slice(h * blk + kw, (h + 1) * blk)

    m_ref[...] = jnp.full_like(m_ref, NEG_BIG)
    l_ref[...] = jnp.zeros_like(l_ref)
    acc_ref[...] = jnp.zeros_like(acc_ref)

    def cols(j):
        return pl.ds(pl.multiple_of(j * kw, kw), kw)

    def scores(j):
        return _dot_nt(q_st, k_ref[0, cols(j), :])

    def softmax(s_ref, p_ref, j, diag_tile):
        if trim and diag_tile == 1:
            chunks = [(c0, (c0 // live) * blk + kw + c0 % live) for c0 in range(0, 2 * live, SOFTMAX_ROWS)]
        else:
            chunks = [(r0, r0) for r0 in range(0, 2 * blk, SOFTMAX_ROWS)]
        for c0, r0 in chunks:
            rr = pl.ds(r0, SOFTMAX_ROWS)
            q0 = r0 % blk
            s = s_ref[pl.ds(c0, SOFTMAX_ROWS), :]
            if bias is not None:
                s = s + bias(r0, j * kw)
            if diag_tile is not None and q0 < (diag_tile + 1) * kw - 1:
                s = jnp.where(diag_tile * kw + col_pos <= q0 + row_pos, s, NEG_BIG)
            m_old = m_ref[rr, :]
            m_new = jnp.maximum(m_old, jnp.max(_lane_fold(s, jnp.maximum), axis=1, keepdims=True))
            alpha = jnp.exp2(m_old - m_new)
            alpha_ref[rr, :] = alpha
            p = jnp.exp2(s - _lane_tile(m_new, kw))
            l_ref[rr, :] = alpha * l_ref[rr, :] + _lane_fold(p)
            m_ref[rr, :] = m_new
            p_ref[pl.ds(c0, SOFTMAX_ROWS), :] = p.astype(jnp.bfloat16)

    def stage(j, par, diag_tile=None):
        pv = _dot(p_refs[1 - par][...], v_ref[0, cols(jnp.maximum(j - 1, 0)), :])
        if trim and diag_tile == 0:
            q_live = jnp.concatenate([q_st[live_rows(0)], q_st[live_rows(1)]], axis=0)
            s_refs[1 - par][0:2 * live, :] = _dot_nt(q_live, k_ref[0, cols(j + 1), :])
        elif diag_tile is None or diag_tile + 1 < n_diag:
            s_refs[1 - par][...] = scores(j + 1)
        softmax(s_refs[par], p_refs[par], j, diag_tile)
        if trim and diag_tile == 1:
            last = _dot(p_refs[par][0:2 * live, :], v_ref[0, cols(j), :])
            for h in range(2):
                dead, lv = slice(h * blk, h * blk + kw), live_rows(h)
                acc_ref[dead, :] = acc_ref[dead, :] + pv[dead]
                acc_ref[lv, :] = (alpha_ref[lv, :] * (acc_ref[lv, :] + pv[lv])
                                  + last[h * live:(h + 1) * live])
            return
        acc_ref[...] = alpha_ref[...] * (acc_ref[...] + pv)
        if diag_tile is not None and diag_tile + 1 == n_diag:
            acc_ref[...] += _dot(p_refs[par][...], v_ref[0, cols(j), :])

    def by_parity(j, fn):
        for par in range(2):
            pl.when(j % 2 == par)(functools.partial(fn, par))

    def prologue(par):
        s_refs[par][...] = scores(j_start)
        p_refs[1 - par][...] = jnp.zeros_like(p_refs[1 - par])

    if isinstance(j_start, int) and j_start == 0 and n_diag % 2 == 0:
        prologue(0)

        def pair(jj, carry):
            stage(2 * jj, 0)
            stage(2 * jj + 1, 1)
            return carry

        lax.fori_loop(0, n_full // 2, pair, 0)
        for t in range(n_diag):
            stage(n_full + t, t % 2, t)
    else:
        by_parity(j_start, prologue)

        def body(j, carry):
            by_parity(j, lambda par: stage(j, par))
            return carry

        lax.fori_loop(j_start, n_full, body, 0)
        for t in range(n_diag):
            by_parity(n_full + t, lambda par, t=t: stage(n_full + t, par, t))


def _attn_a_kernel(lam_ref, sg_ref, q_ref, k_ref, v_ref, g_ref, o_ref, *scratch, blk, kw, lam_init):
    seq = q_ref.shape[1]
    low = _head_masks(blk)
    l_ref, acc_ref = scratch[1], scratch[2]
    lv = lam_ref[...]
    lam = (jnp.exp(jnp.sum(lv[0:1] * lv[1:2], axis=1, keepdims=True))
           - jnp.exp(jnp.sum(lv[2:3] * lv[3:4], axis=1, keepdims=True)) + lam_init)

    def q_block(i, _):
        rows = pl.ds(pl.multiple_of(i * blk, blk), blk)
        q_st = _stack_heads(q_ref[0, rows, :], low)
        _causal_flash(i, 0, q_st, k_ref, v_ref, scratch, blk=blk, kw=kw)
        o_st = acc_ref[...] / jnp.sum(l_ref[...], axis=1, keepdims=True)
        o = o_st[0:blk] - lam * o_st[blk:2 * blk]
        o = o * lax.rsqrt(jnp.mean(o * o, axis=1, keepdims=True) + SUBLN_EPS)
        o = o * (sg_ref[...] * (1.0 - lam_init))
        o_ref[0, rows, :] = (o * g_ref[0, rows, :].astype(jnp.float32)).astype(o_ref.dtype)
        return 0

    lax.fori_loop(0, seq // blk, q_block, 0)


def _attn_a(lam_vecs, subln_g, qa, ka, va, ga, lam_init, blk, kw):
    bsz, seq, width = qa.shape
    heads = width // LANES
    assert seq % blk == 0 and blk % kw == 0
    spec = pl.BlockSpec((1, seq, LANES), lambda b, h: (b, 0, h))
    small = lambda a: pl.BlockSpec(a.shape, lambda b, h: (0, 0))
    return pl.pallas_call(
        functools.partial(_attn_a_kernel, blk=blk, kw=kw, lam_init=lam_init),
        grid=(bsz, heads),
        in_specs=[small(lam_vecs), small(subln_g), spec, spec, spec, spec],
        out_specs=spec,
        out_shape=jax.ShapeDtypeStruct((bsz, seq, width), jnp.bfloat16),
        scratch_shapes=_flash_scratch(blk, kw),
        compiler_params=pltpu.CompilerParams(
            dimension_semantics=("arbitrary", "arbitrary"), vmem_limit_bytes=VMEM_LIMIT),
        name="attn_diff",
    )(lam_vecs, subln_g, qa, ka, va, ga)


def _attn_b_kernel(q_ref, k_ref, v_ref, g_ref, cc_ref, cr_ref, ce_ref, o_ref, cq_ref, *scratch, blk, kw):
    seq = q_ref.shape[1]
    pair = pl.program_id(1)
    low = _head_masks(blk)
    lane = lax.broadcasted_iota(jnp.int32, (1, LANES), 1)
    l_ref, acc_ref = scratch[1], scratch[2]
    f32 = jnp.float32

    def kn_block(j, kn2):
        kk = k_ref[0, pl.ds(pl.multiple_of(j * blk, blk), blk), :].astype(f32)
        sq = kk * kk
        zero = jnp.zeros_like(sq)
        n_lo = jnp.max(jnp.sum(jnp.where(low, sq, zero), axis=1, keepdims=True), axis=0, keepdims=True)
        n_hi = jnp.max(jnp.sum(jnp.where(low, zero, sq), axis=1, keepdims=True), axis=0, keepdims=True)
        return jnp.maximum(kn2[0], n_lo), jnp.maximum(kn2[1], n_hi)

    kn2 = lax.fori_loop(0, seq // blk, kn_block, (jnp.zeros((1, 1), f32), jnp.zeros((1, 1), f32)))
    kn = (jnp.sqrt(kn2[0]), jnp.sqrt(kn2[1]))

    def q_block(i, _):
        rows = pl.ds(pl.multiple_of(i * blk, blk), blk)
        q_st = _stack_heads(q_ref[0, rows, :], low)
        qf = q_st.astype(f32)
        kf = k_ref[0, rows, :].astype(f32)
        qn = jnp.sqrt(jnp.sum(qf * qf, axis=1, keepdims=True))
        diag = jnp.sum(qf * jnp.concatenate([kf, kf], axis=0), axis=1, keepdims=True)
        cc = cc_ref[0, rows, :]
        cq = jnp.concatenate(
            [jnp.sum(jnp.where(lane == SUBLANES * pair + h, cc, 0.0), axis=1, keepdims=True)
             for h in range(2)], axis=0)
        cq_ref[...] = jnp.broadcast_to(cq, cq_ref.shape)

        n_full = (i * blk) // kw
        j_start = n_full
        for h in range(2):
            hr = slice(h * blk, (h + 1) * blk)
            reach = jnp.max(qn[hr] * kn[h] + cq[hr] - diag[hr], axis=0, keepdims=True)
            skipped = (ce_ref[0, 0, h:h + 1, :] > reach + SKIP_NATS * LOG2E).astype(jnp.int32)
            j_start = jnp.minimum(j_start, jnp.sum(skipped))

        def bias(r0, start):
            h = r0 // blk
            ck = cr_ref[0, 0, h:h + 1, pl.ds(pl.multiple_of(start, LANES), kw)]
            return _lane_tile(cq_ref[pl.ds(r0, SOFTMAX_ROWS), :], kw) - ck

        _causal_flash(i, j_start, q_st, k_ref, v_ref, scratch, blk=blk, kw=kw, bias=bias)
        o_st = acc_ref[...] / jnp.sum(l_ref[...], axis=1, keepdims=True)
        o = jnp.where(low, o_st[0:blk], o_st[blk:2 * blk])
        o_ref[0, rows, :] = (o * g_ref[0, rows, :].astype(f32)).astype(o_ref.dtype)
        return 0

    lax.fori_loop(0, seq // blk, q_block, 0)


def _attn_b(qb, kb, vb, gb, c_col, c_row, c_end, blk, kw):
    bsz, seq, width = qb.shape
    pairs = width // LANES
    assert seq % blk == 0 and blk % kw == 0
    spec = pl.BlockSpec((1, seq, LANES), lambda b, h: (b, 0, h))
    per_pair = lambda a: pl.BlockSpec((1, 1) + a.shape[2:], lambda b, h: (b, h, 0, 0))
    return pl.pallas_call(
        functools.partial(_attn_b_kernel, blk=blk, kw=kw),
        grid=(bsz, pairs),
        in_specs=[spec, spec, spec, spec,
                  pl.BlockSpec((1, seq, LANES), lambda b, h: (b, 0, 0)),
                  per_pair(c_row), per_pair(c_end)],
        out_specs=spec,
        out_shape=jax.ShapeDtypeStruct((bsz, seq, width), jnp.bfloat16),
        scratch_shapes=[pltpu.VMEM((2 * blk, LANES), jnp.float32)] + _flash_scratch(blk, kw),
        compiler_params=pltpu.CompilerParams(
            dimension_semantics=("arbitrary", "arbitrary"), vmem_limit_bytes=VMEM_LIMIT),
        name="attn_forget",
    )(qb, kb, vb, gb, c_col, c_row, c_end)


def _attn_c_kernel(q_ref, k_ref, v_ref, g_ref, o_ref, run_ref, acc_ref, z_ref, hl_ref, p_ref, *, blk):
    seq = q_ref.shape[1]
    nsub = blk // LANES
    low = _head_masks(blk)
    f32, bf16 = jnp.float32, jnp.bfloat16
    jj = lax.broadcasted_iota(jnp.int32, (2 * LANES, LANES), 0) % LANES
    ss = lax.broadcasted_iota(jnp.int32, (2 * LANES, LANES), 1)
    after_op = (jj > ss).astype(bf16)
    row_pos = lax.broadcasted_iota(jnp.int32, (SOFTMAX_ROWS, blk), 0)
    col_pos = lax.broadcasted_iota(jnp.int32, (SOFTMAX_ROWS, blk), 1)
    row_all = lax.broadcasted_iota(jnp.int32, (2 * blk, LANES), 0) % blk
    lane_all = lax.broadcasted_iota(jnp.int32, (2 * blk, LANES), 1)

    def q_block(i, _):
        rows = pl.ds(pl.multiple_of(i * blk, blk), blk)
        q_st = _stack_heads(q_ref[0, rows, :], low)
        run_ref[...] = jnp.zeros_like(run_ref)
        acc_ref[...] = jnp.zeros_like(acc_ref)

        def step(j, masked):
            cols = pl.ds(pl.multiple_of(j * blk, blk), blk)
            z_ref[...] = _dot_nt(q_st, k_ref[0, cols, :])
            for r0 in range(0, 2 * blk, SOFTMAX_ROWS):
                rr = pl.ds(r0, SOFTMAX_ROWS)
                z = z_ref[rr, :]
                miss = jnp.maximum(z, 0.0) + jnp.log2(1.0 + jnp.exp2(-jnp.abs(z)))
                log_hit = z - miss
                if masked:
                    miss = jnp.where(col_pos < (r0 % blk) + row_pos, miss, 0.0)
                hi = miss.astype(bf16)
                lo = (miss - hi.astype(f32)).astype(bf16)
                off = run_ref[rr, :]
                shifted = [None] * nsub
                for b in reversed(range(nsub)):
                    sl = slice(b * LANES, (b + 1) * LANES)
                    hl_ref[rr, 2 * b * LANES:(2 * b + 1) * LANES] = hi[:, sl]
                    hl_ref[rr, (2 * b + 1) * LANES:(2 * b + 2) * LANES] = lo[:, sl]
                    shifted[b] = log_hit[:, sl] - off
                    off = off + jnp.sum(miss[:, sl], axis=1, keepdims=True)
                run_ref[rr, :] = off
                z_ref[rr, :] = jnp.concatenate(shifted, axis=1)
            for b in range(nsub):
                sl = slice(b * LANES, (b + 1) * LANES)
                inside = _dot(hl_ref[:, 2 * b * LANES:(2 * b + 2) * LANES], after_op)
                w = jnp.exp2(z_ref[:, sl] - inside)
                if masked:
                    w = jnp.where(b * LANES + lane_all < row_all, w, 0.0)
                p_ref[:, sl] = w.astype(bf16)
            acc_ref[...] += _dot(p_ref[...], v_ref[0, cols, :])
            return (jnp.min(run_ref[...]) <= SKIP_NATS * LOG2E).astype(jnp.int32)

        alive = step(i, True)
        lax.while_loop(lambda c: jnp.logical_and(c[0] >= 0, c[1] > 0),
                       lambda c: (c[0] - 1, step(c[0], False)), (i - 1, alive))

        o = jnp.where(low, acc_ref[0:blk], acc_ref[blk:2 * blk])
        o_ref[0, rows, :] = (o * g_ref[0, rows, :].astype(f32)).astype(o_ref.dtype)
        return 0

    lax.fori_loop(0, seq // blk, q_block, 0)


def _attn_c(qc, kc, vc, gc, blk):
    bsz, seq, width = qc.shape
    pairs = width // LANES
    assert seq % blk == 0
    spec = pl.BlockSpec((1, seq, LANES), lambda b, h: (b, 0, h))
    return pl.pallas_call(
        functools.partial(_attn_c_kernel, blk=blk),
        grid=(bsz, pairs),
        in_specs=[spec, spec, spec, spec],
        out_specs=spec,
        out_shape=jax.ShapeDtypeStruct((bsz, seq, width), jnp.bfloat16),
        scratch_shapes=[pltpu.VMEM((2 * blk, LANES), jnp.float32), pltpu.VMEM((2 * blk, LANES), jnp.float32),
                        pltpu.VMEM((2 * blk, blk), jnp.float32), pltpu.VMEM((2 * blk, 2 * blk), jnp.bfloat16),
                        pltpu.VMEM((2 * blk, blk), jnp.bfloat16)],
        compiler_params=pltpu.CompilerParams(
            dimension_semantics=("arbitrary", "arbitrary"), vmem_limit_bytes=VMEM_LIMIT),
        name="attn_stick",
    )(qc, kc, vc, gc)


def _outproj_kernel(ya_ref, yb_ref, yc_ref, h_ref, w_ref, g_ref, b_ref, o_ref, *, alpha):
    wa, wb = ya_ref.shape[1], yb_ref.shape[1]
    y = (_dot(ya_ref[...], w_ref[0:wa, :]) + _dot(yb_ref[...], w_ref[wa:wa + wb, :])
         + _dot(yc_ref[...], w_ref[wa + wb:, :]))
    o_ref[...] = _layernorm_rows(alpha * h_ref[...] + y, g_ref[...], b_ref[...])


def _outproj(ya, yb, yc, h2d, w, g, b, alpha, tm):
    m, d = h2d.shape
    row = lambda a: pl.BlockSpec((tm, a.shape[1]), lambda i: (i, 0))
    full = lambda a: pl.BlockSpec(a.shape, lambda i: (0, 0))
    return pl.pallas_call(
        functools.partial(_outproj_kernel, alpha=alpha),
        grid=(m // tm,),
        in_specs=[row(ya), row(yb), row(yc), row(h2d), full(w), full(g), full(b)],
        out_specs=pl.BlockSpec((tm, d), lambda i: (i, 0)),
        out_shape=jax.ShapeDtypeStruct((m, d), jnp.float32),
        compiler_params=pltpu.CompilerParams(vmem_limit_bytes=VMEM_LIMIT),
        name="out_proj_ln",
    )(ya, yb, yc, h2d, w, g, b)


def _rope_tables(seq):
    half = ROT_DIM // 2
    inv = ROPE_THETA ** (-jnp.arange(0, ROT_DIM, 2, dtype=jnp.float32) / ROT_DIM)
    ang = jnp.arange(seq, dtype=jnp.float32)[:, None] * inv[None, :]
    cos, sin = jnp.cos(ang), jnp.sin(ang)
    pad = HEAD_DIM - ROT_DIM
    one_head = lambda a, b, fill: jnp.concatenate(
        [a, b, jnp.full((seq, pad), fill, jnp.float32)], axis=1)
    zeros = jnp.zeros((seq, half), jnp.float32)
    reps = LANES // HEAD_DIM
    cos_t = jnp.tile(one_head(cos, cos, 1.0), (1, reps))
    sin_hi = jnp.tile(one_head(zeros, sin, 0.0), (1, reps))
    sin_lo = jnp.tile(one_head(-sin, zeros, 0.0), (1, reps))
    return cos_t, sin_hi, sin_lo


def kernel(x, ln_in_g, ln_in_b, w_in, b_forget, lambda_q1, lambda_k1, lambda_q2, lambda_k2,
           subln_g, w_out, ln_g, ln_b):
    bsz, seq, d = x.shape
    depth = w_in.shape[0]
    hb = b_forget.shape[1]
    d_mix = w_out.shape[1]
    wa, wb, wc = d_mix // 2, d_mix // 4, d_mix // 4
    assert w_in.shape[2] == 4 * wa + 4 * wb + hb + 4 * wc and wb == hb * HEAD_DIM
    alpha = (2 * depth) ** 0.25
    tm = min(512, seq)
    blk = min(512, seq)
    blk_a = min(1024, seq)
    kw = min(512, seq)
    bf16 = jnp.bfloat16

    cos, shi, slo = _rope_tables(seq)
    tri = (jnp.arange(tm)[:, None] >= jnp.arange(tm)[None, :]).astype(bf16)

    h = x
    for layer in range(depth):
        w = w_in[layer]
        o_b = 4 * wa
        o_f = o_b + 3 * wb
        o_g = o_f + hb
        o_c = o_g + wb
        spread = lambda a: jnp.pad(a.reshape(a.shape[:-1] + (hb // 2, 2)),
                                   [(0, 0)] * (a.ndim - 1)
                                   + [(0, LANES // SUBLANES - hb // 2), (0, SUBLANES - 2)]
                                   ).reshape(a.shape[:-1] + (LANES,))
        w_a, w_b, w_f, w_c, w_o = _cast_bf16(
            [w[:, :o_b], jnp.concatenate([w[:, o_b:o_f], w[:, o_g:o_c]], axis=1),
             spread(w[:, o_f:o_g]), w[:, o_c:], w_out[layer]], min(256, d))
        b_f = spread(b_forget[layer]).reshape(1, LANES)

        outs = _inproj(h, w_a, w_b, w_f, w_c, b_f, cos, shi, slo, tri, tm,
                       ln=(ln_in_g, ln_in_b) if layer == 0 else None)
        if layer == 0:
            h = outs[-1]
        (qa, ka, va, ga, qb, kb, vb, gb, c_col, c_row, qc, kc, vc, gc) = outs[:14]
        c_end = c_row[:, :, :, kw - 1::kw]

        lam_init = 0.8 - 0.6 * math.exp(-0.3 * layer)
        lam_vecs = jnp.stack([lambda_q1[layer], lambda_k1[layer], lambda_q2[layer], lambda_k2[layer]])
        ya = _attn_a(lam_vecs, subln_g[layer].reshape(1, LANES), qa, ka, va, ga, lam_init, blk_a, kw)
        yb = _attn_b(qb, kb, vb, gb, c_col, c_row, c_end, blk, kw)
        yc = _attn_c(qc, kc, vc, gc, blk)

        m = bsz * seq
        h = _outproj(ya.reshape(m, wa), yb.reshape(m, wb), yc.reshape(m, wc), h.reshape(m, d),
                     w_o, ln_g[layer].reshape(1, d), ln_b[layer].reshape(1, d),
                     alpha, tm).reshape(bsz, seq, d)
    return h
```

```python
import functools
import math

import jax
import jax.numpy as jnp
from jax import lax
from jax.experimental import pallas as pl
from jax.experimental.pallas import tpu as pltpu

HEAD_DIM = 64
LANES = 128
SUBLANES = 8
ROT_DIM = HEAD_DIM // 4
ROPE_THETA = 500000.0
LN_EPS = 1e-5
SUBLN_EPS = 1e-5
LOG2E = 1.4426950408889634
NEG_BIG = -1e30
SKIP_NATS = 100.0
VMEM_LIMIT = 56 * 1024 * 1024
SOFTMAX_ROWS = 32

_NT = (((1,), (1,)), ((), ()))


def _dot(a, b):
    return jnp.dot(a, b, preferred_element_type=jnp.float32)


def _dot_nt(a, b):
    return lax.dot_general(a, b, _NT, preferred_element_type=jnp.float32)


def _layernorm_rows(t, g, b):
    mu = jnp.mean(t, axis=-1, keepdims=True)
    d = t - mu
    var = jnp.mean(d * d, axis=-1, keepdims=True)
    return d * lax.rsqrt(var + LN_EPS) * g + b


def _cast_kernel(*refs):
    n = len(refs) // 2
    for src, dst in zip(refs[:n], refs[n:]):
        dst[...] = src[...].astype(dst.dtype)


def _cast_bf16(arrays, tr):
    rows = arrays[0].shape[0]
    spec = lambda a: pl.BlockSpec((tr, a.shape[1]), lambda i: (i, 0))
    return pl.pallas_call(
        _cast_kernel,
        grid=(rows // tr,),
        in_specs=[spec(a) for a in arrays],
        out_specs=[spec(a) for a in arrays],
        out_shape=[jax.ShapeDtypeStruct(a.shape, jnp.bfloat16) for a in arrays],
        compiler_params=pltpu.CompilerParams(vmem_limit_bytes=VMEM_LIMIT),
        name="cast_weights",
    )(*arrays)


def _rope(t, cos, sin_hi, sin_lo):
    out = []
    for c in range(t.shape[1] // LANES):
        x = t[:, c * LANES:(c + 1) * LANES]
        out.append(x * cos
                   + pltpu.roll(x, ROT_DIM // 2, axis=1) * sin_hi
                   + pltpu.roll(x, LANES - ROT_DIM // 2, axis=1) * sin_lo)
    return jnp.concatenate(out, axis=1)


def _silu(t):
    return t * jax.nn.sigmoid(t)


def _softplus(t):
    return jnp.maximum(t, 0.0) + jnp.log1p(jnp.exp(-jnp.abs(t)))


def _inproj_kernel(*refs, wa, wb, wc, pre_ln):
    if pre_ln:
        g_ref, b_ref, *refs = refs
    (h_ref, wa_ref, wb_ref, wf_ref, wc_ref, bf_ref, cos_ref, shi_ref, slo_ref, tri_ref,
     qa_ref, ka_ref, va_ref, ga_ref, qb_ref, kb_ref, vb_ref, gb_ref, cb_ref, cr_ref,
     qc_ref, kc_ref, vc_ref, gc_ref, *rest) = refs
    carry_ref = rest[-1]
    scale = HEAD_DIM ** -0.5
    h = h_ref[0]
    if pre_ln:
        h = _layernorm_rows(h, g_ref[...], b_ref[...])
        rest[0][0] = h
    hb = h.astype(jnp.bfloat16)
    cos, shi, slo = cos_ref[...], shi_ref[...], slo_ref[...]
    bf16 = jnp.bfloat16

    qa_ref[0] = (_rope(_dot(hb, wa_ref[:, 0:wa]), cos, shi, slo) * (scale * LOG2E)).astype(bf16)
    ka_ref[0] = _rope(_dot(hb, wa_ref[:, wa:2 * wa]), cos, shi, slo).astype(bf16)
    va_ref[0] = _dot(hb, wa_ref[:, 2 * wa:3 * wa]).astype(bf16)
    ga_ref[0] = _silu(_dot(hb, wa_ref[:, 3 * wa:4 * wa])).astype(bf16)

    qb_ref[0] = (_dot(hb, wb_ref[:, 0:wb]) * (scale * LOG2E)).astype(bf16)
    kb_ref[0] = _dot(hb, wb_ref[:, wb:2 * wb]).astype(bf16)
    vb_ref[0] = _dot(hb, wb_ref[:, 2 * wb:3 * wb]).astype(bf16)
    gb_ref[0] = _silu(_dot(hb, wb_ref[:, 3 * wb:4 * wb])).astype(bf16)

    @pl.when(pl.program_id(1) == 0)
    def _():
        carry_ref[...] = jnp.zeros_like(carry_ref)

    logf = -_softplus(-(_dot(hb, wf_ref[...]) + bf_ref[...]))
    p1 = logf.astype(bf16)
    r1 = logf - p1.astype(jnp.float32)
    p2 = r1.astype(bf16)
    p3 = (r1 - p2.astype(jnp.float32)).astype(bf16)
    cs = _dot(tri_ref[...], jnp.concatenate([p1, p2, p3], axis=1))
    c = cs[:, 0:LANES] + cs[:, LANES:2 * LANES] + cs[:, 2 * LANES:3 * LANES] + carry_ref[...]
    carry_ref[...] = c[c.shape[0] - 1:, :]
    c2 = c * LOG2E
    cb_ref[0] = c2
    c2t = c2.T
    for p in range(cr_ref.shape[1]):
        cr_ref[0, p] = c2t[SUBLANES * p:SUBLANES * (p + 1), :]

    qc_ref[0] = (_dot(hb, wc_ref[:, 0:wc]) * (scale * LOG2E)).astype(bf16)
    kc_ref[0] = _dot(hb, wc_ref[:, wc:2 * wc]).astype(bf16)
    vc_ref[0] = _dot(hb, wc_ref[:, 2 * wc:3 * wc]).astype(bf16)
    gc_ref[0] = _silu(_dot(hb, wc_ref[:, 3 * wc:4 * wc])).astype(bf16)


def _inproj(h, w_a, w_b, w_f, w_c, b_f, cos, shi, slo, tri, tm, ln=None):
    bsz, seq, d = h.shape
    wa, wb, wc = w_a.shape[1] // 4, w_b.shape[1] // 4, w_c.shape[1] // 4
    row = lambda n: pl.BlockSpec((1, tm, n), lambda b, t: (b, t, 0))
    full = lambda a: pl.BlockSpec(a.shape, lambda b, t: (0,) * a.ndim)
    tab = pl.BlockSpec((tm, LANES), lambda b, t: (t, 0))
    bf16 = jnp.bfloat16
    shp = lambda n, dt=bf16: jax.ShapeDtypeStruct((bsz, seq, n), dt)
    pairs = wb // LANES
    c_row_spec = pl.BlockSpec((1, pairs, SUBLANES, tm), lambda b, t: (b, 0, 0, t))
    c_row_shape = jax.ShapeDtypeStruct((bsz, pairs, SUBLANES, seq), jnp.float32)
    ln_args = [] if ln is None else [a.reshape(1, d) for a in ln]
    return pl.pallas_call(
        functools.partial(_inproj_kernel, wa=wa, wb=wb, wc=wc, pre_ln=ln is not None),
        grid=(bsz, seq // tm),
        in_specs=([full(a) for a in ln_args]
                  + [row(d), full(w_a), full(w_b), full(w_f), full(w_c), full(b_f), tab, tab, tab, full(tri)]),
        out_specs=([row(wa)] * 4 + [row(wb)] * 4 + [row(LANES), c_row_spec] + [row(wc)] * 4
                   + ([] if ln is None else [row(d)])),
        out_shape=([shp(wa)] * 4 + [shp(wb)] * 4 + [shp(LANES, jnp.float32), c_row_shape]
                   + [shp(wc)] * 4 + ([] if ln is None else [shp(d, jnp.float32)])),
        scratch_shapes=[pltpu.VMEM((1, LANES), jnp.float32)],
        compiler_params=pltpu.CompilerParams(
            dimension_semantics=("arbitrary", "arbitrary"), vmem_limit_bytes=VMEM_LIMIT),
        name="in_proj",
    )(*ln_args, h, w_a, w_b, w_f, w_c, b_f, cos, shi, slo, tri)


def _head_masks(rows):
    lane = lax.broadcasted_iota(jnp.int32, (rows, LANES), 1)
    return lane < HEAD_DIM


def _stack_heads(q, low):
    zero = jnp.zeros_like(q)
    return jnp.concatenate([jnp.where(low, q, zero), jnp.where(low, zero, q)], axis=0)


def _lane_tile(t, width):
    return jnp.concatenate([t] * (width // LANES), axis=1)


def _lane_fold(t, op=jnp.add):
    out = t[:, 0:LANES]
    for c in range(1, t.shape[1] // LANES):
        out = op(out, t[:, c * LANES:(c + 1) * LANES])
    return out


def _flash_scratch(blk, kw):
    return ([pltpu.VMEM((2 * blk, LANES), jnp.float32)] * 4
            + [pltpu.VMEM((2 * blk, kw), jnp.float32)] * 2
            + [pltpu.VMEM((2 * blk, kw), jnp.bfloat16)] * 2)


def _causal_flash(i, j_start, q_st, k_ref, v_ref, scratch, *, blk, kw, bias=None):
    m_ref, l_ref, acc_ref, alpha_ref, s0_ref, s1_ref, p0_ref, p1_ref = scratch
    s_refs, p_refs = (s0_ref, s1_ref), (p0_ref, p1_ref)
    row_pos = lax.broadcasted_iota(jnp.int32, (SOFTMAX_ROWS, kw), 0)
    col_pos = lax.broadcasted_iota(jnp.int32, (SOFTMAX_ROWS, kw), 1)
    n_diag = blk // kw
    n_full = i * n_diag
    trim = n_diag == 2
    live = blk - kw

    def live_rows(h):
        return slice(h * blk + kw, (h + 1) * blk)

    m_ref[...] = jnp.full_like(m_ref, NEG_BIG)
    l_ref[...] = jnp.zeros_like(l_ref)
    acc_ref[...] = jnp.zeros_like(acc_ref)

    def cols(j):
        return pl.ds(pl.multiple_of(j * kw, kw), kw)

    def scores(j):
        return _dot_nt(q_st, k_ref[0, cols(j), :])

    def softmax(s_ref, p_ref, j, diag_tile):
        if trim and diag_tile == 1:
            chunks = [(c0, (c0 // live) * blk + kw + c0 % live) for c0 in range(0, 2 * live, SOFTMAX_ROWS)]
        else:
            chunks = [(r0, r0) for r0 in range(0, 2 * blk, SOFTMAX_ROWS)]
        for c0, r0 in chunks:
            rr = pl.ds(r0, SOFTMAX_ROWS)
            q0 = r0 % blk
            s = s_ref[pl.ds(c0, SOFTMAX_ROWS), :]
            if bias is not None:
                s = s + bias(r0, j * kw)
            if diag_tile is not None and q0 < (diag_tile + 1) * kw - 1:
                s = jnp.where(diag_tile * kw + col_pos <= q0 + row_pos, s, NEG_BIG)
            m_old = m_ref[rr, :]
            m_new = jnp.maximum(m_old, jnp.max(_lane_fold(s, jnp.maximum), axis=1, keepdims=True))
            alpha = jnp.exp2(m_old - m_new)
            alpha_ref[rr, :] = alpha
            p = jnp.exp2(s - _lane_tile(m_new, kw))
            l_ref[rr, :] = alpha * l_ref[rr, :] + _lane_fold(p)
            m_ref[rr, :] = m_new
            p_ref[pl.ds(c0, SOFTMAX_ROWS), :] = p.astype(jnp.bfloat16)

    def stage(j, par, diag_tile=None):
        pv = _dot(p_refs[1 - par][...], v_ref[0, cols(jnp.maximum(j - 1, 0)), :])
        if trim and diag_tile == 0:
            q_live = jnp.concatenate([q_st[live_rows(0)], q_st[live_rows(1)]], axis=0)
            s_refs[1 - par][0:2 * live, :] = _dot_nt(q_live, k_ref[0, cols(j + 1), :])
        elif diag_tile is None or diag_tile + 1 < n_diag:
            s_refs[1 - par][...] = scores(j + 1)
        softmax(s_refs[par], p_refs[par], j, diag_tile)
        if trim and diag_tile == 1:
            last = _dot(p_refs[par][0:2 * live, :], v_ref[0, cols(j), :])
            for h in range(2):
                dead, lv = slice(h * blk, h * blk + kw), live_rows(h)
                acc_ref[dead, :] = acc_ref[dead, :] + pv[dead]
                acc_ref[lv, :] = (alpha_ref[lv, :] * (acc_ref[lv, :] + pv[lv])
                                  + last[h * live:(h + 1) * live])
            return
        acc_ref[...] = alpha_ref[...] * (acc_ref[...] + pv)
        if diag_tile is not None and diag_tile + 1 == n_diag:
            acc_ref[...] += _dot(p_refs[par][...], v_ref[0, cols(j), :])

    def by_parity(j, fn):
        for par in range(2):
            pl.when(j % 2 == par)(functools.partial(fn, par))

    def prologue(par):
        s_refs[par][...] = scores(j_start)
        p_refs[1 - par][...] = jnp.zeros_like(p_refs[1 - par])

    if isinstance(j_start, int) and j_start == 0 and n_diag % 2 == 0:
        prologue(0)

        def pair(jj, carry):
            stage(2 * jj, 0)
            stage(2 * jj + 1, 1)
            return carry

        lax.fori_loop(0, n_full // 2, pair, 0)
        for t in range(n_diag):
            stage(n_full + t, t % 2, t)
    else:
        by_parity(j_start, prologue)

        def body(j, carry):
            by_parity(j, lambda par: stage(j, par))
            return carry

        lax.fori_loop(j_start, n_full, body, 0)
        for t in range(n_diag):
            by_parity(n_full + t, lambda par, t=t: stage(n_full + t, par, t))


def _attn_a_kernel(lam_ref, sg_ref, q_ref, k_ref, v_ref, g_ref, o_ref, *scratch, blk, kw, lam_init):
    seq = q_ref.shape[1]
    low = _head_masks(blk)
    l_ref, acc_ref = scratch[1], scratch[2]
    lv = lam_ref[...]
    lam = (jnp.exp(jnp.sum(lv[0:1] * lv[1:2], axis=1, keepdims=True))
           - jnp.exp(jnp.sum(lv[2:3] * lv[3:4], axis=1, keepdims=True)) + lam_init)

    def q_block(i, _):
        rows = pl.ds(pl.multiple_of(i * blk, blk), blk)
        q_st = _stack_heads(q_ref[0, rows, :], low)
        _causal_flash(i, 0, q_st, k_ref, v_ref, scratch, blk=blk, kw=kw)
        o_st = acc_ref[...] / jnp.sum(l_ref[...], axis=1, keepdims=True)
        o = o_st[0:blk] - lam * o_st[blk:2 * blk]
        o = o * lax.rsqrt(jnp.mean(o * o, axis=1, keepdims=True) + SUBLN_EPS)
        o = o * (sg_ref[...] * (1.0 - lam_init))
        o_ref[0, rows, :] = (o * g_ref[0, rows, :].astype(jnp.float32)).astype(o_ref.dtype)
        return 0

    lax.fori_loop(0, seq // blk, q_block, 0)


def _attn_a(lam_vecs, subln_g, qa, ka, va, ga, lam_init, blk, kw):
    bsz, seq, width = qa.shape
    heads = width // LANES
    assert seq % blk == 0 and blk % kw == 0
    spec = pl.BlockSpec((1, seq, LANES), lambda b, h: (b, 0, h))
    small = lambda a: pl.BlockSpec(a.shape, lambda b, h: (0, 0))
    return pl.pallas_call(
        functools.partial(_attn_a_kernel, blk=blk, kw=kw, lam_init=lam_init),
        grid=(bsz, heads),
        in_specs=[small(lam_vecs), small(subln_g), spec, spec, spec, spec],
        out_specs=spec,
        out_shape=jax.ShapeDtypeStruct((bsz, seq, width), jnp.bfloat16),
        scratch_shapes=_flash_scratch(blk, kw),
        compiler_params=pltpu.CompilerParams(
            dimension_semantics=("arbitrary", "arbitrary"), vmem_limit_bytes=VMEM_LIMIT),
        name="attn_diff",
    )(lam_vecs, subln_g, qa, ka, va, ga)


def _attn_b_kernel(q_ref, k_ref, v_ref, g_ref, cc_ref, cr_ref, ce_ref, o_ref, cq_ref, *scratch, blk, kw):
    seq = q_ref.shape[1]
    pair = pl.program_id(1)
    low = _head_masks(blk)
    lane = lax.broadcasted_iota(jnp.int32, (1, LANES), 1)
    l_ref, acc_ref = scratch[1], scratch[2]
    f32 = jnp.float32

    def kn_block(j, kn2):
        kk = k_ref[0, pl.ds(pl.multiple_of(j * blk, blk), blk), :].astype(f32)
        sq = kk * kk
        zero = jnp.zeros_like(sq)
        n_lo = jnp.max(jnp.sum(jnp.where(low, sq, zero), axis=1, keepdims=True), axis=0, keepdims=True)
        n_hi = jnp.max(jnp.sum(jnp.where(low, zero, sq), axis=1, keepdims=True), axis=0, keepdims=True)
        return jnp.maximum(kn2[0], n_lo), jnp.maximum(kn2[1], n_hi)

    kn2 = lax.fori_loop(0, seq // blk, kn_block, (jnp.zeros((1, 1), f32), jnp.zeros((1, 1), f32)))
    kn = (jnp.sqrt(kn2[0]), jnp.sqrt(kn2[1]))

    def q_block(i, _):
        rows = pl.ds(pl.multiple_of(i * blk, blk), blk)
        q_st = _stack_heads(q_ref[0, rows, :], low)
        qf = q_st.astype(f32)
        kf = k_ref[0, rows, :].astype(f32)
        qn = jnp.sqrt(jnp.sum(qf * qf, axis=1, keepdims=True))
        diag = jnp.sum(qf * jnp.concatenate([kf, kf], axis=0), axis=1, keepdims=True)
        cc = cc_ref[0, rows, :]
        cq = jnp.concatenate(
            [jnp.sum(jnp.where(lane == SUBLANES * pair + h, cc, 0.0), axis=1, keepdims=True)
             for h in range(2)], axis=0)
        cq_ref[...] = jnp.broadcast_to(cq, cq_ref.shape)

        n_full = (i * blk) // kw
        j_start = n_full
        for h in range(2):
            hr = slice(h * blk, (h + 1) * blk)
            reach = jnp.max(qn[hr] * kn[h] + cq[hr] - diag[hr], axis=0, keepdims=True)
            skipped = (ce_ref[0, 0, h:h + 1, :] > reach + SKIP_NATS * LOG2E).astype(jnp.int32)
            j_start = jnp.minimum(j_start, jnp.sum(skipped))

        def bias(r0, start):
            h = r0 // blk
            ck = cr_ref[0, 0, h:h + 1, pl.ds(pl.multiple_of(start, LANES), kw)]
            return _lane_tile(cq_ref[pl.ds(r0, SOFTMAX_ROWS), :], kw) - ck

        _causal_flash(i, j_start, q_st, k_ref, v_ref, scratch, blk=blk, kw=kw, bias=bias)
        o_st = acc_ref[...] / jnp.sum(l_ref[...], axis=1, keepdims=True)
        o = jnp.where(low, o_st[0:blk], o_st[blk:2 * blk])
        o_ref[0, rows, :] = (o * g_ref[0, rows, :].astype(f32)).astype(o_ref.dtype)
        return 0

    lax.fori_loop(0, seq // blk, q_block, 0)


def _attn_b(qb, kb, vb, gb, c_col, c_row, c_end, blk, kw):
    bsz, seq, width = qb.shape
    pairs = width // LANES
    assert seq % blk == 0 and blk % kw == 0
    spec = pl.BlockSpec((1, seq, LANES), lambda b, h: (b, 0, h))
    per_pair = lambda a: pl.BlockSpec((1, 1) + a.shape[2:], lambda b, h: (b, h, 0, 0))
    return pl.pallas_call(
        functools.partial(_attn_b_kernel, blk=blk, kw=kw),
        grid=(bsz, pairs),
        in_specs=[spec, spec, spec, spec,
                  pl.BlockSpec((1, seq, LANES), lambda b, h: (b, 0, 0)),
                  per_pair(c_row), per_pair(c_end)],
        out_specs=spec,
        out_shape=jax.ShapeDtypeStruct((bsz, seq, width), jnp.bfloat16),
        scratch_shapes=[pltpu.VMEM((2 * blk, LANES), jnp.float32)] + _flash_scratch(blk, kw),
        compiler_params=pltpu.CompilerParams(
            dimension_semantics=("arbitrary", "arbitrary"), vmem_limit_bytes=VMEM_LIMIT),
        name="attn_forget",
    )(qb, kb, vb, gb, c_col, c_row, c_end)


def _attn_c_kernel(q_ref, k_ref, v_ref, g_ref, o_ref, run_ref, acc_ref, *, blk):
    seq = q_ref.shape[1]
    sub = LANES
    n_sub = blk // sub
    low = _head_masks(sub)
    f32, bf16 = jnp.float32, jnp.bfloat16
    jj = lax.broadcasted_iota(jnp.int32, (2 * sub, sub), 0) % sub
    ss = lax.broadcasted_iota(jnp.int32, (2 * sub, sub), 1)
    after_op = (jj > ss).astype(bf16)
    strict = ss < jj
    gone = 2.0 * SKIP_NATS * LOG2E

    def q_block(i, _):
        run_ref[...] = jnp.zeros_like(run_ref)
        acc_ref[...] = jnp.zeros_like(acc_ref)

        def step(d, masked):
            kbs = [i * n_sub + g - d for g in range(n_sub)]
            has_keys = [jnp.where(kb >= 0, 1.0, 0.0) for kb in kbs]
            keys = [pl.ds(pl.multiple_of(jnp.maximum(kb, 0) * sub, sub), sub) for kb in kbs]
            state = [slice(2 * g * sub, 2 * (g + 1) * sub) for g in range(n_sub)]
            zs = []
            for g in range(n_sub):
                q_rows = pl.ds(pl.multiple_of(i * blk + g * sub, sub), sub)
                zs.append(_dot_nt(_stack_heads(q_ref[0, q_rows, :], low), k_ref[0, keys[g], :]))
            insides, hits = [], []
            for g in range(n_sub):
                z = zs[g]
                miss = jnp.maximum(z, 0.0) + jnp.log2(1.0 + jnp.exp2(-jnp.abs(z)))
                if masked:
                    miss = jnp.where(strict, miss, 0.0)
                hi = miss.astype(bf16)
                lo = (miss - hi.astype(f32)).astype(bf16)
                insides.append(_dot(jnp.concatenate([hi, lo], axis=1), after_op))
                run = run_ref[state[g], :]
                hits.append((z - miss) - run)
                total = jnp.sum(miss, axis=1, keepdims=True)
                run_ref[state[g], :] = run + (total * has_keys[g] + gone * (1.0 - has_keys[g]))
            for g in range(n_sub):
                w = jnp.exp2(hits[g] - insides[g])
                w = jnp.where(strict, w, 0.0) if masked else w * has_keys[g]
                acc_ref[state[g], :] += _dot(w.astype(bf16), v_ref[0, keys[g], :])
            return (jnp.min(run_ref[...]) <= SKIP_NATS * LOG2E).astype(jnp.int32)

        alive = step(0, True)
        lax.while_loop(lambda c: c[1] > 0, lambda c: (c[0] + 1, step(c[0], False)), (1, alive))

        for g in range(n_sub):
            q_rows = pl.ds(pl.multiple_of(i * blk + g * sub, sub), sub)
            o = jnp.where(low, acc_ref[2 * g * sub:(2 * g + 1) * sub], acc_ref[(2 * g + 1) * sub:2 * (g + 1) * sub])
            o_ref[0, q_rows, :] = (o * g_ref[0, q_rows, :].astype(f32)).astype(o_ref.dtype)
        return 0

    lax.fori_loop(0, seq // blk, q_block, 0)


def _attn_c(qc, kc, vc, gc, blk):
    bsz, seq, width = qc.shape
    pairs = width // LANES
    assert seq % blk == 0
    spec = pl.BlockSpec((1, seq, LANES), lambda b, h: (b, 0, h))
    return pl.pallas_call(
        functools.partial(_attn_c_kernel, blk=blk),
        grid=(bsz, pairs),
        in_specs=[spec, spec, spec, spec],
        out_specs=spec,
        out_shape=jax.ShapeDtypeStruct((bsz, seq, width), jnp.bfloat16),
        scratch_shapes=[pltpu.VMEM((2 * blk, LANES), jnp.float32), pltpu.VMEM((2 * blk, LANES), jnp.float32)],
        compiler_params=pltpu.CompilerParams(
            dimension_semantics=("arbitrary", "arbitrary"), vmem_limit_bytes=VMEM_LIMIT),
        name="attn_stick",
    )(qc, kc, vc, gc)


def _outproj_kernel(ya_ref, yb_ref, yc_ref, h_ref, w_ref, g_ref, b_ref, o_ref, *, alpha):
    wa, wb = ya_ref.shape[1], yb_ref.shape[1]
    y = (_dot(ya_ref[...], w_ref[0:wa, :]) + _dot(yb_ref[...], w_ref[wa:wa + wb, :])
         + _dot(yc_ref[...], w_ref[wa + wb:, :]))
    o_ref[...] = _layernorm_rows(alpha * h_ref[...] + y, g_ref[...], b_ref[...])


def _outproj(ya, yb, yc, h2d, w, g, b, alpha, tm):
    m, d = h2d.shape
    row = lambda a: pl.BlockSpec((tm, a.shape[1]), lambda i: (i, 0))
    full = lambda a: pl.BlockSpec(a.shape, lambda i: (0, 0))
    return pl.pallas_call(
        functools.partial(_outproj_kernel, alpha=alpha),
        grid=(m // tm,),
        in_specs=[row(ya), row(yb), row(yc), row(h2d), full(w), full(g), full(b)],
        out_specs=pl.BlockSpec((tm, d), lambda i: (i, 0)),
        out_shape=jax.ShapeDtypeStruct((m, d), jnp.float32),
        compiler_params=pltpu.CompilerParams(vmem_limit_bytes=VMEM_LIMIT),
        name="out_proj_ln",
    )(ya, yb, yc, h2d, w, g, b)


def _rope_tables(seq):
    half = ROT_DIM // 2
    inv = ROPE_THETA ** (-jnp.arange(0, ROT_DIM, 2, dtype=jnp.float32) / ROT_DIM)
    ang = jnp.arange(seq, dtype=jnp.float32)[:, None] * inv[None, :]
    cos, sin = jnp.cos(ang), jnp.sin(ang)
    pad = HEAD_DIM - ROT_DIM
    one_head = lambda a, b, fill: jnp.concatenate(
        [a, b, jnp.full((seq, pad), fill, jnp.float32)], axis=1)
    zeros = jnp.zeros((seq, half), jnp.float32)
    reps = LANES // HEAD_DIM
    cos_t = jnp.tile(one_head(cos, cos, 1.0), (1, reps))
    sin_hi = jnp.tile(one_head(zeros, sin, 0.0), (1, reps))
    sin_lo = jnp.tile(one_head(-sin, zeros, 0.0), (1, reps))
    return cos_t, sin_hi, sin_lo


def kernel(x, ln_in_g, ln_in_b, w_in, b_forget, lambda_q1, lambda_k1, lambda_q2, lambda_k2,
           subln_g, w_out, ln_g, ln_b):
    bsz, seq, d = x.shape
    depth = w_in.shape[0]
    hb = b_forget.shape[1]
    d_mix = w_out.shape[1]
    wa, wb, wc = d_mix // 2, d_mix // 4, d_mix // 4
    assert w_in.shape[2] == 4 * wa + 4 * wb + hb + 4 * wc and wb == hb * HEAD_DIM
    alpha = (2 * depth) ** 0.25
    tm = min(512, seq)
    blk = min(512, seq)
    blk_a = min(1024, seq)
    kw = min(512, seq)
    bf16 = jnp.bfloat16

    cos, shi, slo = _rope_tables(seq)
    tri = (jnp.arange(tm)[:, None] >= jnp.arange(tm)[None, :]).astype(bf16)

    h = x
    for layer in range(depth):
        w = w_in[layer]
        o_b = 4 * wa
        o_f = o_b + 3 * wb
        o_g = o_f + hb
        o_c = o_g + wb
        spread = lambda a: jnp.pad(a.reshape(a.shape[:-1] + (hb // 2, 2)),
                                   [(0, 0)] * (a.ndim - 1)
                                   + [(0, LANES // SUBLANES - hb // 2), (0, SUBLANES - 2)]
                                   ).reshape(a.shape[:-1] + (LANES,))
        w_a, w_b, w_f, w_c, w_o = _cast_bf16(
            [w[:, :o_b], jnp.concatenate([w[:, o_b:o_f], w[:, o_g:o_c]], axis=1),
             spread(w[:, o_f:o_g]), w[:, o_c:], w_out[layer]], min(256, d))
        b_f = spread(b_forget[layer]).reshape(1, LANES)

        outs = _inproj(h, w_a, w_b, w_f, w_c, b_f, cos, shi, slo, tri, tm,
                       ln=(ln_in_g, ln_in_b) if layer == 0 else None)
        if layer == 0:
            h = outs[-1]
        (qa, ka, va, ga, qb, kb, vb, gb, c_col, c_row, qc, kc, vc, gc) = outs[:14]
        c_end = c_row[:, :, :, kw - 1::kw]

        lam_init = 0.8 - 0.6 * math.exp(-0.3 * layer)
        lam_vecs = jnp.stack([lambda_q1[layer], lambda_k1[layer], lambda_q2[layer], lambda_k2[layer]])
        ya = _attn_a(lam_vecs, subln_g[layer].reshape(1, LANES), qa, ka, va, ga, lam_init, blk_a, kw)
        yb = _attn_b(qb, kb, vb, gb, c_col, c_row, c_end, blk, kw)
        yc = _attn_c(qc, kc, vc, gc, blk_a)

        m = bsz * seq
        h = _outproj(ya.reshape(m, wa), yb.reshape(m, wb), yc.reshape(m, wc), h.reshape(m, d),
                     w_o, ln_g[layer].reshape(1, d), ln_b[layer].reshape(1, d),
                     alpha, tm).reshape(bsz, seq, d)
    return h
```

```python
import functools
import math

import jax
import jax.numpy as jnp
from jax import lax
from jax.experimental import pallas as pl
from jax.experimental.pallas import tpu as pltpu

HEAD_DIM = 64
LANES = 128
SUBLANES = 8
ROT_DIM = HEAD_DIM // 4
ROPE_THETA = 500000.0
LN_EPS = 1e-5
SUBLN_EPS = 1e-5
LOG2E = 1.4426950408889634
NEG_BIG = -1e30
SKIP_NATS = 100.0
VMEM_LIMIT = 56 * 1024 * 1024
SOFTMAX_ROWS = 32

_NT = (((1,), (1,)), ((), ()))


def _dot(a, b):
    return jnp.dot(a, b, preferred_element_type=jnp.float32)


def _dot_nt(a, b):
    return lax.dot_general(a, b, _NT, preferred_element_type=jnp.float32)


def _layernorm_rows(t, g, b):
    mu = jnp.mean(t, axis=-1, keepdims=True)
    d = t - mu
    var = jnp.mean(d * d, axis=-1, keepdims=True)
    return d * lax.rsqrt(var + LN_EPS) * g + b


def _cast_kernel(*refs):
    n = len(refs) // 2
    for src, dst in zip(refs[:n], refs[n:]):
        dst[...] = src[...].astype(dst.dtype)


def _cast_bf16(arrays, tr):
    rows = arrays[0].shape[0]
    spec = lambda a: pl.BlockSpec((tr, a.shape[1]), lambda i: (i, 0))
    return pl.pallas_call(
        _cast_kernel,
        grid=(rows // tr,),
        in_specs=[spec(a) for a in arrays],
        out_specs=[spec(a) for a in arrays],
        out_shape=[jax.ShapeDtypeStruct(a.shape, jnp.bfloat16) for a in arrays],
        compiler_params=pltpu.CompilerParams(vmem_limit_bytes=VMEM_LIMIT),
        name="cast_weights",
    )(*arrays)


def _rope(t, cos, sin_hi, sin_lo):
    out = []
    for c in range(t.shape[1] // LANES):
        x = t[:, c * LANES:(c + 1) * LANES]
        out.append(x * cos
                   + pltpu.roll(x, ROT_DIM // 2, axis=1) * sin_hi
                   + pltpu.roll(x, LANES - ROT_DIM // 2, axis=1) * sin_lo)
    return jnp.concatenate(out, axis=1)


def _silu(t):
    return t * jax.nn.sigmoid(t)


def _softplus(t):
    return jnp.maximum(t, 0.0) + jnp.log1p(jnp.exp(-jnp.abs(t)))


def _inproj_kernel(*refs, wa, wb, wc, pre_ln):
    if pre_ln:
        g_ref, b_ref, *refs = refs
    (h_ref, wa_ref, wb_ref, wf_ref, wc_ref, bf_ref, cos_ref, shi_ref, slo_ref, tri_ref,
     qa_ref, ka_ref, va_ref, ga_ref, qb_ref, kb_ref, vb_ref, gb_ref, cb_ref, cr_ref,
     qc_ref, kc_ref, vc_ref, gc_ref, *rest) = refs
    carry_ref = rest[-1]
    scale = HEAD_DIM ** -0.5
    h = h_ref[0]
    if pre_ln:
        h = _layernorm_rows(h, g_ref[...], b_ref[...])
        rest[0][0] = h
    hb = h.astype(jnp.bfloat16)
    cos, shi, slo = cos_ref[...], shi_ref[...], slo_ref[...]
    bf16 = jnp.bfloat16

    qa_ref[0] = (_rope(_dot(hb, wa_ref[:, 0:wa]), cos, shi, slo) * (scale * LOG2E)).astype(bf16)
    ka_ref[0] = _rope(_dot(hb, wa_ref[:, wa:2 * wa]), cos, shi, slo).astype(bf16)
    va_ref[0] = _dot(hb, wa_ref[:, 2 * wa:3 * wa]).astype(bf16)
    ga_ref[0] = _silu(_dot(hb, wa_ref[:, 3 * wa:4 * wa])).astype(bf16)

    qb_ref[0] = (_dot(hb, wb_ref[:, 0:wb]) * (scale * LOG2E)).astype(bf16)
    kb_ref[0] = _dot(hb, wb_ref[:, wb:2 * wb]).astype(bf16)
    vb_ref[0] = _dot(hb, wb_ref[:, 2 * wb:3 * wb]).astype(bf16)
    gb_ref[0] = _silu(_dot(hb, wb_ref[:, 3 * wb:4 * wb])).astype(bf16)

    @pl.when(pl.program_id(1) == 0)
    def _():
        carry_ref[...] = jnp.zeros_like(carry_ref)

    logf = -_softplus(-(_dot(hb, wf_ref[...]) + bf_ref[...]))
    p1 = logf.astype(bf16)
    r1 = logf - p1.astype(jnp.float32)
    p2 = r1.astype(bf16)
    p3 = (r1 - p2.astype(jnp.float32)).astype(bf16)
    cs = _dot(tri_ref[...], jnp.concatenate([p1, p2, p3], axis=1))
    c = cs[:, 0:LANES] + cs[:, LANES:2 * LANES] + cs[:, 2 * LANES:3 * LANES] + carry_ref[...]
    carry_ref[...] = c[c.shape[0] - 1:, :]
    c2 = c * LOG2E
    cb_ref[0] = c2
    c2t = c2.T
    for p in range(cr_ref.shape[1]):
        cr_ref[0, p] = c2t[SUBLANES * p:SUBLANES * (p + 1), :]

    qc_ref[0] = (_dot(hb, wc_ref[:, 0:wc]) * (scale * LOG2E)).astype(bf16)
    kc_ref[0] = _dot(hb, wc_ref[:, wc:2 * wc]).astype(bf16)
    vc_ref[0] = _dot(hb, wc_ref[:, 2 * wc:3 * wc]).astype(bf16)
    gc_ref[0] = _silu(_dot(hb, wc_ref[:, 3 * wc:4 * wc])).astype(bf16)


def _inproj(h, w_a, w_b, w_f, w_c, b_f, cos, shi, slo, tri, tm, ln=None):
    bsz, seq, d = h.shape
    wa, wb, wc = w_a.shape[1] // 4, w_b.shape[1] // 4, w_c.shape[1] // 4
    row = lambda n: pl.BlockSpec((1, tm, n), lambda b, t: (b, t, 0))
    full = lambda a: pl.BlockSpec(a.shape, lambda b, t: (0,) * a.ndim)
    tab = pl.BlockSpec((tm, LANES), lambda b, t: (t, 0))
    bf16 = jnp.bfloat16
    shp = lambda n, dt=bf16: jax.ShapeDtypeStruct((bsz, seq, n), dt)
    pairs = wb // LANES
    c_row_spec = pl.BlockSpec((1, pairs, SUBLANES, tm), lambda b, t: (b, 0, 0, t))
    c_row_shape = jax.ShapeDtypeStruct((bsz, pairs, SUBLANES, seq), jnp.float32)
    ln_args = [] if ln is None else [a.reshape(1, d) for a in ln]
    return pl.pallas_call(
        functools.partial(_inproj_kernel, wa=wa, wb=wb, wc=wc, pre_ln=ln is not None),
        grid=(bsz, seq // tm),
        in_specs=([full(a) for a in ln_args]
                  + [row(d), full(w_a), full(w_b), full(w_f), full(w_c), full(b_f), tab, tab, tab, full(tri)]),
        out_specs=([row(wa)] * 4 + [row(wb)] * 4 + [row(LANES), c_row_spec] + [row(wc)] * 4
                   + ([] if ln is None else [row(d)])),
        out_shape=([shp(wa)] * 4 + [shp(wb)] * 4 + [shp(LANES, jnp.float32), c_row_shape]
                   + [shp(wc)] * 4 + ([] if ln is None else [shp(d, jnp.float32)])),
        scratch_shapes=[pltpu.VMEM((1, LANES), jnp.float32)],
        compiler_params=pltpu.CompilerParams(
            dimension_semantics=("arbitrary", "arbitrary"), vmem_limit_bytes=VMEM_LIMIT),
        name="in_proj",
    )(*ln_args, h, w_a, w_b, w_f, w_c, b_f, cos, shi, slo, tri)


def _head_masks(rows):
    lane = lax.broadcasted_iota(jnp.int32, (rows, LANES), 1)
    return lane < HEAD_DIM


def _stack_heads(q, low):
    zero = jnp.zeros_like(q)
    return jnp.concatenate([jnp.where(low, q, zero), jnp.where(low, zero, q)], axis=0)


def _lane_tile(t, width):
    return jnp.concatenate([t] * (width // LANES), axis=1)


def _lane_fold(t, op=jnp.add):
    out = t[:, 0:LANES]
    for c in range(1, t.shape[1] // LANES):
        out = op(out, t[:, c * LANES:(c + 1) * LANES])
    return out


def _flash_scratch(blk, kw):
    return ([pltpu.VMEM((2 * blk, LANES), jnp.float32)] * 4
            + [pltpu.VMEM((2 * blk, kw), jnp.float32)] * 2
            + [pltpu.VMEM((2 * blk, kw), jnp.bfloat16)] * 2)


def _causal_flash(i, j_start, q_st, k_ref, v_ref, scratch, *, blk, kw, bias=None):
    m_ref, l_ref, acc_ref, alpha_ref, s0_ref, s1_ref, p0_ref, p1_ref = scratch
    s_refs, p_refs = (s0_ref, s1_ref), (p0_ref, p1_ref)
    row_pos = lax.broadcasted_iota(jnp.int32, (SOFTMAX_ROWS, kw), 0)
    col_pos = lax.broadcasted_iota(jnp.int32, (SOFTMAX_ROWS, kw), 1)
    n_diag = blk // kw
    n_full = i * n_diag
    trim = n_diag == 2
    live = blk - kw

    def live_rows(h):
        return slice(h * blk + kw, (h + 1) * blk)

    m_ref[...] = jnp.full_like(m_ref, NEG_BIG)
    l_ref[...] = jnp.zeros_like(l_ref)
    acc_ref[...] = jnp.zeros_like(acc_ref)

    def cols(j):
        return pl.ds(pl.multiple_of(j * kw, kw), kw)

    def scores(j):
        return _dot_nt(q_st, k_ref[0, cols(j), :])

    def softmax(s_ref, p_ref, j, diag_tile):
        if trim and diag_tile == 1:
            chunks = [(c0, (c0 // live) * blk + kw + c0 % live) for c0 in range(0, 2 * live, SOFTMAX_ROWS)]
        else:
            chunks = [(r0, r0) for r0 in range(0, 2 * blk, SOFTMAX_ROWS)]
        for c0, r0 in chunks:
            rr = pl.ds(r0, SOFTMAX_ROWS)
            q0 = r0 % blk
            s = s_ref[pl.ds(c0, SOFTMAX_ROWS), :]
            if bias is not None:
                s = s + bias(r0, j * kw)
            if diag_tile is not None and q0 < (diag_tile + 1) * kw - 1:
                s = jnp.where(diag_tile * kw + col_pos <= q0 + row_pos, s, NEG_BIG)
            m_old = m_ref[rr, :]
            m_new = jnp.maximum(m_old, jnp.max(_lane_fold(s, jnp.maximum), axis=1, keepdims=True))
            alpha = jnp.exp2(m_old - m_new)
            alpha_ref[rr, :] = alpha
            p = jnp.exp2(s - _lane_tile(m_new, kw))
            l_ref[rr, :] = alpha * l_ref[rr, :] + _lane_fold(p)
            m_ref[rr, :] = m_new
            p_ref[pl.ds(c0, SOFTMAX_ROWS), :] = p.astype(jnp.bfloat16)

    def stage(j, par, diag_tile=None):
        pv = _dot(p_refs[1 - par][...], v_ref[0, cols(jnp.maximum(j - 1, 0)), :])
        if trim and diag_tile == 0:
            q_live = jnp.concatenate([q_st[live_rows(0)], q_st[live_rows(1)]], axis=0)
            s_refs[1 - par][0:2 * live, :] = _dot_nt(q_live, k_ref[0, cols(j + 1), :])
        elif diag_tile is None or diag_tile + 1 < n_diag:
            s_refs[1 - par][...] = scores(j + 1)
        softmax(s_refs[par], p_refs[par], j, diag_tile)
        if trim and diag_tile == 1:
            last = _dot(p_refs[par][0:2 * live, :], v_ref[0, cols(j), :])
            for h in range(2):
                dead, lv = slice(h * blk, h * blk + kw), live_rows(h)
                acc_ref[dead, :] = acc_ref[dead, :] + pv[dead]
                acc_ref[lv, :] = (alpha_ref[lv, :] * (acc_ref[lv, :] + pv[lv])
                                  + last[h * live:(h + 1) * live])
            return
        acc_ref[...] = alpha_ref[...] * (acc_ref[...] + pv)
        if diag_tile is not None and diag_tile + 1 == n_diag:
            acc_ref[...] += _dot(p_refs[par][...], v_ref[0, cols(j), :])

    def by_parity(j, fn):
        for par in range(2):
            pl.when(j % 2 == par)(functools.partial(fn, par))

    def prologue(par):
        s_refs[par][...] = scores(j_start)
        p_refs[1 - par][...] = jnp.zeros_like(p_refs[1 - par])

    if isinstance(j_start, int) and j_start == 0 and n_diag % 2 == 0:
        prologue(0)

        def pair(jj, carry):
            stage(2 * jj, 0)
            stage(2 * jj + 1, 1)
            return carry

        lax.fori_loop(0, n_full // 2, pair, 0)
        for t in range(n_diag):
            stage(n_full + t, t % 2, t)
    else:
        by_parity(j_start, prologue)

        def body(j, carry):
            by_parity(j, lambda par: stage(j, par))
            return carry

        lax.fori_loop(j_start, n_full, body, 0)
        for t in range(n_diag):
            by_parity(n_full + t, lambda par, t=t: stage(n_full + t, par, t))


def _attn_a_kernel(lam_ref, sg_ref, q_ref, k_ref, v_ref, g_ref, o_ref, *scratch, blk, kw, lam_init):
    seq = q_ref.shape[1]
    low = _head_masks(blk)
    l_ref, acc_ref = scratch[1], scratch[2]
    lv = lam_ref[...]
    lam = (jnp.exp(jnp.sum(lv[0:1] * lv[1:2], axis=1, keepdims=True))
           - jnp.exp(jnp.sum(lv[2:3] * lv[3:4], axis=1, keepdims=True)) + lam_init)

    def q_block(i, _):
        rows = pl.ds(pl.multiple_of(i * blk, blk), blk)
        q_st = _stack_heads(q_ref[0, rows, :], low)
        _causal_flash(i, 0, q_st, k_ref, v_ref, scratch, blk=blk, kw=kw)
        o_st = acc_ref[...] / jnp.sum(l_ref[...], axis=1, keepdims=True)
        o = o_st[0:blk] - lam * o_st[blk:2 * blk]
        o = o * lax.rsqrt(jnp.mean(o * o, axis=1, keepdims=True) + SUBLN_EPS)
        o = o * (sg_ref[...] * (1.0 - lam_init))
        o_ref[0, rows, :] = (o * g_ref[0, rows, :].astype(jnp.float32)).astype(o_ref.dtype)
        return 0

    lax.fori_loop(0, seq // blk, q_block, 0)


def _attn_a(lam_vecs, subln_g, qa, ka, va, ga, lam_init, blk, kw):
    bsz, seq, width = qa.shape
    heads = width // LANES
    assert seq % blk == 0 and blk % kw == 0
    spec = pl.BlockSpec((1, seq, LANES), lambda b, h: (b, 0, h))
    small = lambda a: pl.BlockSpec(a.shape, lambda b, h: (0, 0))
    return pl.pallas_call(
        functools.partial(_attn_a_kernel, blk=blk, kw=kw, lam_init=lam_init),
        grid=(bsz, heads),
        in_specs=[small(lam_vecs), small(subln_g), spec, spec, spec, spec],
        out_specs=spec,
        out_shape=jax.ShapeDtypeStruct((bsz, seq, width), jnp.bfloat16),
        scratch_shapes=_flash_scratch(blk, kw),
        compiler_params=pltpu.CompilerParams(
            dimension_semantics=("arbitrary", "arbitrary"), vmem_limit_bytes=VMEM_LIMIT),
        name="attn_diff",
    )(lam_vecs, subln_g, qa, ka, va, ga)


def _attn_b_kernel(q_ref, k_ref, v_ref, g_ref, cc_ref, cr_ref, o_ref, m_ref, l_ref, acc_ref, cq_ref, reach_ref,
                   *, blk):
    seq = q_ref.shape[1]
    sub = LANES
    n_sub = blk // sub
    pair = pl.program_id(1)
    low = _head_masks(blk)
    low_sub = _head_masks(sub)
    lane = lax.broadcasted_iota(jnp.int32, (1, LANES), 1)
    jj = lax.broadcasted_iota(jnp.int32, (2 * sub, sub), 0) % sub
    ss = lax.broadcasted_iota(jnp.int32, (2 * sub, sub), 1)
    causal = ss <= jj
    f32, bf16 = jnp.float32, jnp.bfloat16
    state = [slice(2 * g * sub, 2 * (g + 1) * sub) for g in range(n_sub)]

    def per_head(lo_val, hi_val):
        n = lo_val.shape[1]
        return jnp.concatenate([jnp.broadcast_to(lo_val, (sub, n)), jnp.broadcast_to(hi_val, (sub, n))], axis=0)

    def kn_block(j, kn2):
        kk = k_ref[0, pl.ds(pl.multiple_of(j * blk, blk), blk), :].astype(f32)
        sq = kk * kk
        zero = jnp.zeros_like(sq)
        n_lo = jnp.max(jnp.sum(jnp.where(low, sq, zero), axis=1, keepdims=True), axis=0, keepdims=True)
        n_hi = jnp.max(jnp.sum(jnp.where(low, zero, sq), axis=1, keepdims=True), axis=0, keepdims=True)
        return jnp.maximum(kn2[0], n_lo), jnp.maximum(kn2[1], n_hi)

    kn2 = lax.fori_loop(0, seq // blk, kn_block, (jnp.zeros((1, 1), f32), jnp.zeros((1, 1), f32)))
    kn = (jnp.sqrt(kn2[0]), jnp.sqrt(kn2[1]))

    kn_rows = per_head(kn[0], kn[1])
    gone = 2.0 * SKIP_NATS * LOG2E

    def q_rows(i, g):
        return pl.ds(pl.multiple_of(i * blk + g * sub, sub), sub)

    def q_block(i, _):
        m_ref[...] = jnp.full_like(m_ref, NEG_BIG)
        l_ref[...] = jnp.zeros_like(l_ref)
        acc_ref[...] = jnp.zeros_like(acc_ref)
        for g in range(n_sub):
            qf = _stack_heads(q_ref[0, q_rows(i, g), :], low_sub).astype(f32)
            qn = jnp.sqrt(jnp.sum(qf * qf, axis=1, keepdims=True))
            cc = cc_ref[0, q_rows(i, g), :]
            cq = jnp.concatenate(
                [jnp.sum(jnp.where(lane == SUBLANES * pair + h, cc, 0.0), axis=1, keepdims=True)
                 for h in range(2)], axis=0)
            cq_ref[state[g], :] = jnp.broadcast_to(cq, (2 * sub, LANES))
            reach_ref[state[g], :] = jnp.broadcast_to(qn * kn_rows + cq, (2 * sub, LANES))

        def step(d, masked):
            kbs = [i * n_sub + g - d for g in range(n_sub)]
            has_keys = [jnp.where(kb >= 0, 1.0, 0.0) for kb in kbs]
            keys = [pl.ds(pl.multiple_of(jnp.maximum(kb, 0) * sub, sub), sub) for kb in kbs]
            scores = [_dot_nt(_stack_heads(q_ref[0, q_rows(i, g), :], low_sub), k_ref[0, keys[g], :])
                      for g in range(n_sub)]
            probs, alphas = [], []
            alive = jnp.full((1, 1), -1.0, f32)
            for g in range(n_sub):
                st = state[g]
                ck = per_head(cr_ref[0, 0, 0:1, keys[g]], cr_ref[0, 0, 1:2, keys[g]])
                s = scores[g] + (cq_ref[st, :] - ck)
                if masked:
                    s = jnp.where(causal, s, NEG_BIG)
                else:
                    s = s * has_keys[g] + NEG_BIG * (1.0 - has_keys[g])
                m_old = m_ref[st, :]
                m_new = jnp.maximum(m_old, jnp.max(s, axis=1, keepdims=True))
                alpha = jnp.exp2(m_old - m_new)
                p = jnp.exp2(s - m_new)
                l_ref[st, :] = alpha * l_ref[st, :] + p
                m_ref[st, :] = m_new
                probs.append(p.astype(bf16))
                alphas.append(alpha)
                slack = reach_ref[st, :] - jnp.max(ck, axis=1, keepdims=True) - m_new
                slack = slack * has_keys[g] - gone * (1.0 - has_keys[g])
                alive = jnp.maximum(alive, jnp.max(jnp.max(slack, axis=1, keepdims=True), axis=0, keepdims=True)
                                    + SKIP_NATS * LOG2E)
            for g in range(n_sub):
                acc_ref[state[g], :] = alphas[g] * acc_ref[state[g], :] + _dot(probs[g], v_ref[0, keys[g], :])
            return (jnp.max(alive) >= 0.0).astype(jnp.int32)

        more = step(0, True)
        lax.while_loop(lambda c: c[1] > 0, lambda c: (c[0] + 1, step(c[0], False)), (1, more))

        for g in range(n_sub):
            st = state[g]
            o_st = acc_ref[st, :] / jnp.sum(l_ref[st, :], axis=1, keepdims=True)
            o = jnp.where(low_sub, o_st[0:sub], o_st[sub:2 * sub])
            o_ref[0, q_rows(i, g), :] = (o * g_ref[0, q_rows(i, g), :].astype(f32)).astype(o_ref.dtype)
        return 0

    lax.fori_loop(0, seq // blk, q_block, 0)


def _attn_b(qb, kb, vb, gb, c_col, c_row, blk):
    bsz, seq, width = qb.shape
    pairs = width // LANES
    assert seq % blk == 0 and blk % LANES == 0
    spec = pl.BlockSpec((1, seq, LANES), lambda b, h: (b, 0, h))
    return pl.pallas_call(
        functools.partial(_attn_b_kernel, blk=blk),
        grid=(bsz, pairs),
        in_specs=[spec, spec, spec, spec,
                  pl.BlockSpec((1, seq, LANES), lambda b, h: (b, 0, 0)),
                  pl.BlockSpec((1, 1) + c_row.shape[2:], lambda b, h: (b, h, 0, 0))],
        out_specs=spec,
        out_shape=jax.ShapeDtypeStruct((bsz, seq, width), jnp.bfloat16),
        scratch_shapes=[pltpu.VMEM((2 * blk, LANES), jnp.float32)] * 5,
        compiler_params=pltpu.CompilerParams(
            dimension_semantics=("arbitrary", "arbitrary"), vmem_limit_bytes=VMEM_LIMIT),
        name="attn_forget",
    )(qb, kb, vb, gb, c_col, c_row)


def _attn_c_kernel(q_ref, k_ref, v_ref, g_ref, o_ref, run_ref, acc_ref, *, blk):
    seq = q_ref.shape[1]
    sub = LANES
    n_sub = blk // sub
    low = _head_masks(sub)
    f32, bf16 = jnp.float32, jnp.bfloat16
    jj = lax.broadcasted_iota(jnp.int32, (2 * sub, sub), 0) % sub
    ss = lax.broadcasted_iota(jnp.int32, (2 * sub, sub), 1)
    after_op = (jj > ss).astype(bf16)
    strict = ss < jj
    gone = 2.0 * SKIP_NATS * LOG2E

    def q_block(i, _):
        run_ref[...] = jnp.zeros_like(run_ref)
        acc_ref[...] = jnp.zeros_like(acc_ref)

        def step(d, masked):
            kbs = [i * n_sub + g - d for g in range(n_sub)]
            has_keys = [jnp.where(kb >= 0, 1.0, 0.0) for kb in kbs]
            keys = [pl.ds(pl.multiple_of(jnp.maximum(kb, 0) * sub, sub), sub) for kb in kbs]
            state = [slice(2 * g * sub, 2 * (g + 1) * sub) for g in range(n_sub)]
            zs = []
            for g in range(n_sub):
                q_rows = pl.ds(pl.multiple_of(i * blk + g * sub, sub), sub)
                zs.append(_dot_nt(_stack_heads(q_ref[0, q_rows, :], low), k_ref[0, keys[g], :]))
            insides, hits = [], []
            for g in range(n_sub):
                z = zs[g]
                miss = jnp.maximum(z, 0.0) + jnp.log2(1.0 + jnp.exp2(-jnp.abs(z)))
                if masked:
                    miss = jnp.where(strict, miss, 0.0)
                hi = miss.astype(bf16)
                lo = (miss - hi.astype(f32)).astype(bf16)
                insides.append(_dot(jnp.concatenate([hi, lo], axis=1), after_op))
                run = run_ref[state[g], :]
                hits.append((z - miss) - run)
                total = jnp.sum(miss, axis=1, keepdims=True)
                run_ref[state[g], :] = run + (total * has_keys[g] + gone * (1.0 - has_keys[g]))
            for g in range(n_sub):
                w = jnp.exp2(hits[g] - insides[g])
                w = jnp.where(strict, w, 0.0) if masked else w * has_keys[g]
                acc_ref[state[g], :] += _dot(w.astype(bf16), v_ref[0, keys[g], :])
            return (jnp.min(run_ref[...]) <= SKIP_NATS * LOG2E).astype(jnp.int32)

        alive = step(0, True)
        lax.while_loop(lambda c: c[1] > 0, lambda c: (c[0] + 1, step(c[0], False)), (1, alive))

        for g in range(n_sub):
            q_rows = pl.ds(pl.multiple_of(i * blk + g * sub, sub), sub)
            o = jnp.where(low, acc_ref[2 * g * sub:(2 * g + 1) * sub], acc_ref[(2 * g + 1) * sub:2 * (g + 1) * sub])
            o_ref[0, q_rows, :] = (o * g_ref[0, q_rows, :].astype(f32)).astype(o_ref.dtype)
        return 0

    lax.fori_loop(0, seq // blk, q_block, 0)


def _attn_c(qc, kc, vc, gc, blk):
    bsz, seq, width = qc.shape
    pairs = width // LANES
    assert seq % blk == 0
    spec = pl.BlockSpec((1, seq, LANES), lambda b, h: (b, 0, h))
    return pl.pallas_call(
        functools.partial(_attn_c_kernel, blk=blk),
        grid=(bsz, pairs),
        in_specs=[spec, spec, spec, spec],
        out_specs=spec,
        out_shape=jax.ShapeDtypeStruct((bsz, seq, width), jnp.bfloat16),
        scratch_shapes=[pltpu.VMEM((2 * blk, LANES), jnp.float32), pltpu.VMEM((2 * blk, LANES), jnp.float32)],
        compiler_params=pltpu.CompilerParams(
            dimension_semantics=("arbitrary", "arbitrary"), vmem_limit_bytes=VMEM_LIMIT),
        name="attn_stick",
    )(qc, kc, vc, gc)


def _outproj_kernel(ya_ref, yb_ref, yc_ref, h_ref, w_ref, g_ref, b_ref, o_ref, *, alpha):
    wa, wb = ya_ref.shape[1], yb_ref.shape[1]
    y = (_dot(ya_ref[...], w_ref[0:wa, :]) + _dot(yb_ref[...], w_ref[wa:wa + wb, :])
         + _dot(yc_ref[...], w_ref[wa + wb:, :]))
    o_ref[...] = _layernorm_rows(alpha * h_ref[...] + y, g_ref[...], b_ref[...])


def _outproj(ya, yb, yc, h2d, w, g, b, alpha, tm):
    m, d = h2d.shape
    row = lambda a: pl.BlockSpec((tm, a.shape[1]), lambda i: (i, 0))
    full = lambda a: pl.BlockSpec(a.shape, lambda i: (0, 0))
    return pl.pallas_call(
        functools.partial(_outproj_kernel, alpha=alpha),
        grid=(m // tm,),
        in_specs=[row(ya), row(yb), row(yc), row(h2d), full(w), full(g), full(b)],
        out_specs=pl.BlockSpec((tm, d), lambda i: (i, 0)),
        out_shape=jax.ShapeDtypeStruct((m, d), jnp.float32),
        compiler_params=pltpu.CompilerParams(vmem_limit_bytes=VMEM_LIMIT),
        name="out_proj_ln",
    )(ya, yb, yc, h2d, w, g, b)


def _rope_tables(seq):
    half = ROT_DIM // 2
    inv = ROPE_THETA ** (-jnp.arange(0, ROT_DIM, 2, dtype=jnp.float32) / ROT_DIM)
    ang = jnp.arange(seq, dtype=jnp.float32)[:, None] * inv[None, :]
    cos, sin = jnp.cos(ang), jnp.sin(ang)
    pad = HEAD_DIM - ROT_DIM
    one_head = lambda a, b, fill: jnp.concatenate(
        [a, b, jnp.full((seq, pad), fill, jnp.float32)], axis=1)
    zeros = jnp.zeros((seq, half), jnp.float32)
    reps = LANES // HEAD_DIM
    cos_t = jnp.tile(one_head(cos, cos, 1.0), (1, reps))
    sin_hi = jnp.tile(one_head(zeros, sin, 0.0), (1, reps))
    sin_lo = jnp.tile(one_head(-sin, zeros, 0.0), (1, reps))
    return cos_t, sin_hi, sin_lo


def kernel(x, ln_in_g, ln_in_b, w_in, b_forget, lambda_q1, lambda_k1, lambda_q2, lambda_k2,
           subln_g, w_out, ln_g, ln_b):
    bsz, seq, d = x.shape
    depth = w_in.shape[0]
    hb = b_forget.shape[1]
    d_mix = w_out.shape[1]
    wa, wb, wc = d_mix // 2, d_mix // 4, d_mix // 4
    assert w_in.shape[2] == 4 * wa + 4 * wb + hb + 4 * wc and wb == hb * HEAD_DIM
    alpha = (2 * depth) ** 0.25
    tm = min(512, seq)
    blk = min(512, seq)
    blk_a = min(1024, seq)
    kw = min(512, seq)
    bf16 = jnp.bfloat16

    cos, shi, slo = _rope_tables(seq)
    tri = (jnp.arange(tm)[:, None] >= jnp.arange(tm)[None, :]).astype(bf16)

    h = x
    for layer in range(depth):
        w = w_in[layer]
        o_b = 4 * wa
        o_f = o_b + 3 * wb
        o_g = o_f + hb
        o_c = o_g + wb
        spread = lambda a: jnp.pad(a.reshape(a.shape[:-1] + (hb // 2, 2)),
                                   [(0, 0)] * (a.ndim - 1)
                                   + [(0, LANES // SUBLANES - hb // 2), (0, SUBLANES - 2)]
                                   ).reshape(a.shape[:-1] + (LANES,))
        w_a, w_b, w_f, w_c, w_o = _cast_bf16(
            [w[:, :o_b], jnp.concatenate([w[:, o_b:o_f], w[:, o_g:o_c]], axis=1),
             spread(w[:, o_f:o_g]), w[:, o_c:], w_out[layer]], min(256, d))
        b_f = spread(b_forget[layer]).reshape(1, LANES)

        outs = _inproj(h, w_a, w_b, w_f, w_c, b_f, cos, shi, slo, tri, tm,
                       ln=(ln_in_g, ln_in_b) if layer == 0 else None)
        if layer == 0:
            h = outs[-1]
        (qa, ka, va, ga, qb, kb, vb, gb, c_col, c_row, qc, kc, vc, gc) = outs[:14]

        lam_init = 0.8 - 0.6 * math.exp(-0.3 * layer)
        lam_vecs = jnp.stack([lambda_q1[layer], lambda_k1[layer], lambda_q2[layer], lambda_k2[layer]])
        ya = _attn_a(lam_vecs, subln_g[layer].reshape(1, LANES), qa, ka, va, ga, lam_init, blk_a, kw)
        yb = _attn_b(qb, kb, vb, gb, c_col, c_row, blk_a)
        yc = _attn_c(qc, kc, vc, gc, blk_a)

        m = bsz * seq
        h = _outproj(ya.reshape(m, wa), yb.reshape(m, wb), yc.reshape(m, wc), h.reshape(m, d),
                     w_o, ln_g[layer].reshape(1, d), ln_b[layer].reshape(1, d),
                     alpha, tm).reshape(bsz, seq, d)
    return h
```
